```python
import jax, jax.numpy as jnp
from jax import lax
import numpy as np

D_MODEL = 2048
BATCH = 8
SEQ = 2048
DEPTH = 1

MLSTM_HEADS = 8
MLSTM_QK_DIM = D_MODEL // 16
MLSTM_V_DIM = D_MODEL // 8
MLSTM_QK_WIDTH = MLSTM_HEADS * MLSTM_QK_DIM
MLSTM_WIDTH = MLSTM_HEADS * MLSTM_V_DIM
CHUNK = 128
CONV_WIDTH = D_MODEL
CONV_K = 3
D_FF = ((8 * D_MODEL // 3 + 255) // 256) * 256
EPS = 1e-6

IN_SIZES = (MLSTM_QK_WIDTH, MLSTM_QK_WIDTH, MLSTM_WIDTH, MLSTM_WIDTH,
            MLSTM_HEADS, MLSTM_HEADS,
            CONV_WIDTH, CONV_WIDTH, CONV_WIDTH,
            D_MODEL, D_MODEL)
IN_WIDTH = sum(IN_SIZES)

kernel_name = "hybrid_mlstm_shortconv_gated_block"


def rmsnorm(x, g):
    xf = x.astype(jnp.float32)
    r = lax.rsqrt(jnp.mean(xf * xf, axis=-1, keepdims=True) + EPS)
    return (xf * r * g.astype(jnp.float32)).astype(x.dtype)


def split_columns(p):
    idx = []
    acc = 0
    for s in IN_SIZES[:-1]:
        acc += s
        idx.append(acc)
    return jnp.split(p, idx, axis=-1)


def mlstm_chunkwise(q, k, v, i_pre, f_pre):
    bsz, s_len, h, dk = q.shape
    dv = v.shape[-1]
    nc = s_len // CHUNK
    f32 = jnp.float32

    def to_chunks(t):
        return t.astype(f32).reshape(bsz, nc, CHUNK, h, -1).transpose(1, 0, 3, 2, 4)

    def gate_chunks(t):
        return t.astype(f32).reshape(bsz, nc, CHUNK, h).transpose(1, 0, 3, 2)

    log_f = jax.nn.log_sigmoid(f_pre.astype(f32))
    xs = (to_chunks(q), to_chunks(k), to_chunks(v), gate_chunks(i_pre), gate_chunks(log_f))
    causal = jnp.tril(jnp.ones((CHUNK, CHUNK), dtype=bool))

    def step(carry, inp):
        c_state, n_state, m_state = carry
        qc, kc, vc, li, lf = inp
        b = jnp.cumsum(lf, axis=-1)
        g = b[..., -1]
        dmat = b[..., :, None] - b[..., None, :] + li[..., None, :]
        dmat = jnp.where(causal, dmat, -jnp.inf)
        m_inter = b + m_state[..., None]
        m_comb = jnp.maximum(m_inter, jnp.max(dmat, axis=-1))
        w = jnp.exp(dmat - m_comb[..., None])
        scores = jnp.einsum('bhtk,bhsk->bhts', qc, kc) * w
        inter = jnp.exp(m_inter - m_comb)
        num = (jnp.einsum('bhts,bhsv->bhtv', scores, vc)
               + inter[..., None] * jnp.einsum('bhtk,bhkv->bhtv', qc, c_state))
        den = jnp.sum(scores, axis=-1) + inter * jnp.einsum('bhtk,bhk->bht', qc, n_state)
        h_out = num / jnp.maximum(jnp.abs(den), jnp.exp(-m_comb))[..., None]
        a = g[..., None] - b + li
        m_new = jnp.maximum(g + m_state, jnp.max(a, axis=-1))
        decay = jnp.exp(g + m_state - m_new)
        wk = jnp.exp(a - m_new[..., None])
        c_new = decay[..., None, None] * c_state + jnp.einsum('bhs,bhsk,bhsv->bhkv', wk, kc, vc)
        n_new = decay[..., None] * n_state + jnp.einsum('bhs,bhsk->bhk', wk, kc)
        return (c_new, n_new, m_new), h_out

    init = (jnp.zeros((bsz, h, dk, dv), f32), jnp.zeros((bsz, h, dk), f32), jnp.zeros((bsz, h), f32))
    _, hs = lax.scan(step, init, xs)
    return hs.transpose(1, 0, 3, 2, 4).reshape(bsz, s_len, h, dv)


def causal_depthwise_conv(u, w):
    s_len = u.shape[1]
    up = jnp.pad(u, ((0, 0), (CONV_K - 1, 0), (0, 0)))
    y = w[0] * up[:, 0:s_len]
    for j in range(1, CONV_K):
        y = y + w[j] * up[:, j:j + s_len]
    return y


def setup_inputs(seed: int = 0) -> dict:
    key = jax.random.key(seed)
    ks = jax.random.split(key, 16)
    f32 = jnp.float32
    nrm = lambda k, shape, scale: (jax.random.normal(k, shape, f32) * scale).astype(f32)
    x = jax.random.normal(ks[0], (BATCH, SEQ, D_MODEL), f32)
    norm_mix = 1.0 + nrm(ks[1], (DEPTH, D_MODEL), 0.02)
    w_in = nrm(ks[2], (DEPTH, D_MODEL, IN_WIDTH), D_MODEL ** -0.5)
    b_igate = nrm(ks[3], (DEPTH, MLSTM_HEADS), 0.1)
    b_fgate = (jnp.linspace(3.0, 6.0, MLSTM_HEADS, dtype=f32)[None, :]
               + nrm(ks[4], (DEPTH, MLSTM_HEADS), 0.1))
    conv_w = nrm(ks[5], (DEPTH, CONV_K, CONV_WIDTH), CONV_K ** -0.5)
    mh_norm = 1.0 + nrm(ks[6], (DEPTH, MLSTM_WIDTH), 0.02)
    w_branch_a = nrm(ks[7], (DEPTH, MLSTM_WIDTH, D_MODEL), MLSTM_WIDTH ** -0.5)
    w_branch_b = nrm(ks[8], (DEPTH, CONV_WIDTH, D_MODEL), CONV_WIDTH ** -0.5)
    w_out = nrm(ks[9], (DEPTH, D_MODEL, D_MODEL), D_MODEL ** -0.5)
    norm_ffn = 1.0 + nrm(ks[10], (DEPTH, D_MODEL), 0.02)
    w_gate = nrm(ks[11], (DEPTH, D_MODEL, D_FF), D_MODEL ** -0.5)
    w_up = nrm(ks[12], (DEPTH, D_MODEL, D_FF), D_MODEL ** -0.5)
    w_down = nrm(ks[13], (DEPTH, D_FF, D_MODEL), D_FF ** -0.5)
    norm_final = 1.0 + nrm(ks[14], (D_MODEL,), 0.02)
    return {"x": x, "norm_mix": norm_mix, "w_in": w_in, "b_igate": b_igate, "b_fgate": b_fgate,
            "conv_w": conv_w, "mh_norm": mh_norm, "w_branch_a": w_branch_a,
            "w_branch_b": w_branch_b, "w_out": w_out, "norm_ffn": norm_ffn,
            "w_gate": w_gate, "w_up": w_up, "w_down": w_down, "norm_final": norm_final}


def reference(x, norm_mix, w_in, b_igate, b_fgate, conv_w, mh_norm, w_branch_a,
              w_branch_b, w_out, norm_ffn, w_gate, w_up, w_down, norm_final):
    bsz, s_len, _ = x.shape
    for l in range(DEPTH):
        h = rmsnorm(x, norm_mix[l])
        proj = jnp.einsum('bsd,de->bse', h, w_in[l])
        q, k, v, o_pre, i_pre, f_pre, cb, cc, cx, ga, gb = split_columns(proj)

        q = q.reshape(bsz, s_len, MLSTM_HEADS, MLSTM_QK_DIM) * (MLSTM_QK_DIM ** -0.5)
        k = k.reshape(bsz, s_len, MLSTM_HEADS, MLSTM_QK_DIM)
        v = v.reshape(bsz, s_len, MLSTM_HEADS, MLSTM_V_DIM)
        hm = mlstm_chunkwise(q, k, v, i_pre + b_igate[l], f_pre + b_fgate[l])
        hm = hm * lax.rsqrt(jnp.mean(hm * hm, axis=-1, keepdims=True) + EPS)
        hm = hm.reshape(bsz, s_len, MLSTM_WIDTH) * mh_norm[l].astype(jnp.float32)
        hm = (jax.nn.sigmoid(o_pre.astype(jnp.float32)) * hm).astype(x.dtype)
        y_a = jnp.einsum('bse,ed->bsd', hm, w_branch_a[l])

        u = causal_depthwise_conv(cc * cx, conv_w[l])
        y_b = jnp.einsum('bse,ed->bsd', cb * u, w_branch_b[l])

        merged = jax.nn.sigmoid(ga) * y_a + jax.nn.sigmoid(gb) * y_b
        x = x + jnp.einsum('bsd,de->bse', merged, w_out[l])

        hf = rmsnorm(x, norm_ffn[l])
        act = jax.nn.silu(jnp.einsum('bsd,df->bsf', hf, w_gate[l])) * jnp.einsum('bsd,df->bsf', hf, w_up[l])
        x = x + jnp.einsum('bsf,fd->bsd', act, w_down[l])
    return rmsnorm(x, norm_final)
```

```python
import functools

import jax
import jax.numpy as jnp
from jax import lax
from jax.experimental import pallas as pl
from jax.experimental.pallas import tpu as pltpu

F32 = jnp.float32
BF16 = jnp.bfloat16

HEADS = 8
CHUNK = 128
CONV_K = 3
EPS = 1e-6
GATE_LANES = 128
VMEM_LIMIT_BYTES = 56 * 1024 * 1024


def _params(semantics):
    return pltpu.CompilerParams(dimension_semantics=semantics,
                                vmem_limit_bytes=VMEM_LIMIT_BYTES)


def _rms_scale(x):
    return lax.rsqrt(jnp.mean(x * x, axis=-1, keepdims=True) + EPS)


def _inproj_kernel(x_ref, g_ref, w_ref, wgate_ref, wgate_t_ref,
                   proj_ref, gcol_ref, grow_ref, h_ref, *, n_q_blocks, q_scale):
    j = pl.program_id(1)

    @pl.when(j == 0)
    def _():
        x = x_ref[...]
        h = (x * _rms_scale(x) * g_ref[...]).astype(BF16)
        h_ref[...] = h
        gcol_ref[...] = jnp.dot(h, wgate_ref[...], preferred_element_type=F32)
        grow_ref[...] = lax.dot_general(wgate_t_ref[...], h, (((1,), (1,)), ((), ())),
                                        preferred_element_type=F32)

    acc = jnp.dot(h_ref[...], w_ref[...], preferred_element_type=F32)

    @pl.when(j < n_q_blocks)
    def _():
        proj_ref[...] = (acc * q_scale).astype(BF16)

    @pl.when(j >= n_q_blocks)
    def _():
        proj_ref[...] = acc.astype(BF16)


def _inproj(x2d, g, w_main, w_gate, w_gate_t, *, qk_width, q_scale, tm=1024, tn=1024):
    t, d = x2d.shape
    n = w_main.shape[1]
    assert t % tm == 0 and n % tn == 0 and qk_width % tn == 0
    kern = functools.partial(_inproj_kernel, n_q_blocks=qk_width // tn, q_scale=q_scale)
    return pl.pallas_call(
        kern,
        grid=(t // tm, n // tn),
        in_specs=[
            pl.BlockSpec((tm, d), lambda i, j: (i, 0)),
            pl.BlockSpec((1, d), lambda i, j: (0, 0)),
            pl.BlockSpec((d, tn), lambda i, j: (0, j)),
            pl.BlockSpec((d, GATE_LANES), lambda i, j: (0, 0)),
            pl.BlockSpec((2 * HEADS, d), lambda i, j: (0, 0)),
        ],
        out_specs=[
            pl.BlockSpec((tm, tn), lambda i, j: (i, j)),
            pl.BlockSpec((tm, GATE_LANES), lambda i, j: (i, 0)),
            pl.BlockSpec((2 * HEADS, tm), lambda i, j: (0, i)),
        ],
        out_shape=[
            jax.ShapeDtypeStruct((t, n), BF16),
            jax.ShapeDtypeStruct((t, GATE_LANES), F32),
            jax.ShapeDtypeStruct((2 * HEADS, t), F32),
        ],
        scratch_shapes=[pltpu.VMEM((tm, d), BF16)],
        compiler_params=_params(("parallel", "arbitrary")),
        name="inproj",
    )(x2d, g, w_main, w_gate, w_gate_t)


def _mlstm_kernel(q_ref, k_ref, v_ref, o_ref, gcol_ref, grow_ref, bias_lane_ref,
                  bias_sub_ref, mh_ref, out_ref, c_ref, n_ref, m_ref, *, dk, dv):
    L = CHUNK

    @pl.when(pl.program_id(1) == 0)
    def _():
        c_ref[...] = jnp.zeros_like(c_ref)
        n_ref[...] = jnp.zeros_like(n_ref)
        m_ref[...] = jnp.zeros_like(m_ref)

    row = lax.broadcasted_iota(jnp.int32, (L, L), 0)
    col = lax.broadcasted_iota(jnp.int32, (L, L), 1)
    causal = col <= row
    lower = causal.astype(F32)
    upper = (row <= col).astype(F32)

    xg_col = gcol_ref[...] + bias_lane_ref[...]
    cum_col = jnp.dot(lower, jax.nn.log_sigmoid(xg_col), precision=lax.Precision.HIGHEST,
                      preferred_element_type=F32)
    xg_row = grow_ref[...] + bias_sub_ref[...]
    cum_row = jnp.dot(jax.nn.log_sigmoid(xg_row[HEADS:, :]), upper,
                      precision=lax.Precision.HIGHEST, preferred_element_type=F32)

    for h in range(HEADS):
        q = q_ref[:, h * dk:(h + 1) * dk]
        k = k_ref[:, h * dk:(h + 1) * dk]
        v = v_ref[:, h * dv:(h + 1) * dv]
        li_row = xg_row[h:h + 1, :]
        b_row = cum_row[h:h + 1, :]
        li_col = xg_col[:, h:h + 1]
        b_col = cum_col[:, HEADS + h:HEADS + h + 1]
        g = cum_row[h:h + 1, L - 1:L]
        m_prev = m_ref[h, 0:1, 0:1]
        n_prev = n_ref[h, 0:1, :]
        c_prev = c_ref[h]

        dmat = jnp.where(causal, b_col - b_row + li_row, -jnp.inf)
        m_inter = b_col + m_prev
        m_comb = jnp.maximum(m_inter, jnp.max(dmat, axis=-1, keepdims=True))
        w = jnp.exp(dmat - m_comb)
        s = lax.dot_general(q, k, (((1,), (1,)), ((), ())), preferred_element_type=F32)
        scores = s * w
        inter = jnp.exp(m_inter - m_comb)
        qf = q.astype(F32)
        num = (jnp.dot(scores.astype(BF16), v, preferred_element_type=F32)
               + inter * jnp.dot(q, c_prev.astype(BF16), preferred_element_type=F32))
        den = (jnp.sum(scores, axis=-1, keepdims=True)
               + inter * jnp.sum(qf * n_prev, axis=-1, keepdims=True))
        h_out = num / jnp.maximum(jnp.abs(den), jnp.exp(-m_comb))

        a_col = g - b_col + li_col
        m_new = jnp.maximum(g + m_prev, jnp.max(a_col, axis=0, keepdims=True))
        decay = jnp.exp(g + m_prev - m_new)
        wk = jnp.exp(a_col - m_new)
        wv = (wk * v.astype(F32)).astype(BF16)
        c_ref[h] = decay * c_prev + lax.dot_general(
            k, wv, (((0,), (0,)), ((), ())), preferred_element_type=F32)
        n_new = decay * n_prev + jnp.sum(wk * k.astype(F32), axis=0, keepdims=True)
        n_ref[h] = jnp.broadcast_to(n_new, n_ref.shape[1:])
        m_ref[h] = jnp.broadcast_to(m_new, m_ref.shape[1:])

        hn = h_out * _rms_scale(h_out) * mh_ref[:, h * dv:(h + 1) * dv]
        gate = jax.nn.sigmoid(o_ref[:, h * dv:(h + 1) * dv].astype(F32))
        out_ref[:, h * dv:(h + 1) * dv] = (gate * hn).astype(out_ref.dtype)


def _mlstm(proj, gcol, grow, bias_lane, bias_sub, mh, *, bsz, s_len, qk_width, v_width):
    t = bsz * s_len
    nc = s_len // CHUNK
    dk, dv = qk_width // HEADS, v_width // HEADS
    assert v_width == 2 * qk_width and s_len % CHUNK == 0
    rowblk = lambda b, c: b * nc + c
    kern = functools.partial(_mlstm_kernel, dk=dk, dv=dv)
    return pl.pallas_call(
        kern,
        grid=(bsz, nc),
        in_specs=[
            pl.BlockSpec((CHUNK, qk_width), lambda b, c: (rowblk(b, c), 0)),
            pl.BlockSpec((CHUNK, qk_width), lambda b, c: (rowblk(b, c), 1)),
            pl.BlockSpec((CHUNK, v_width), lambda b, c: (rowblk(b, c), 1)),
            pl.BlockSpec((CHUNK, v_width), lambda b, c: (rowblk(b, c), 2)),
            pl.BlockSpec((CHUNK, GATE_LANES), lambda b, c: (rowblk(b, c), 0)),
            pl.BlockSpec((2 * HEADS, CHUNK), lambda b, c: (0, rowblk(b, c))),
            pl.BlockSpec((1, GATE_LANES), lambda b, c: (0, 0)),
            pl.BlockSpec((2 * HEADS, 1), lambda b, c: (0, 0)),
            pl.BlockSpec((1, v_width), lambda b, c: (0, 0)),
        ],
        out_specs=pl.BlockSpec((CHUNK, v_width), lambda b, c: (rowblk(b, c), 0)),
        out_shape=jax.ShapeDtypeStruct((t, v_width), BF16),
        scratch_shapes=[
            pltpu.VMEM((HEADS, dk, dv), F32),
            pltpu.VMEM((HEADS, 8, dk), F32),
            pltpu.VMEM((HEADS, 8, 128), F32),
        ],
        compiler_params=_params(("parallel", "arbitrary")),
        name="mlstm",
    )(proj, proj, proj, proj, gcol, grow, bias_lane, bias_sub, mh)


HALO = 16


def _merge_kernel(hm_ref, cb_ref, cc_ref, cx_ref, cc_halo_ref, cx_halo_ref, cw_ref,
                  ga_ref, gb_ref, wa_ref, wb_ref, out_ref, cbu_ref, *, blocks_per_seq, lane_chunk):
    i = pl.program_id(0)
    j = pl.program_id(1)
    tm, width = cbu_ref.shape

    @pl.when(j == 0)
    def _():
        keep_halo = (i % blocks_per_seq != 0).astype(F32)
        rows = lax.broadcasted_iota(jnp.int32, (tm, lane_chunk), 0)
        for c0 in range(0, width, lane_chunk):
            sl = slice(c0, c0 + lane_chunk)
            p = cc_ref[:, sl].astype(F32) * cx_ref[:, sl].astype(F32)
            ph = (cc_halo_ref[:, sl].astype(F32) * cx_halo_ref[:, sl].astype(F32)) * keep_halo
            p1 = jnp.where(rows == 0, ph[HALO - 1:HALO, :], pltpu.roll(p, 1, axis=0))
            p2 = jnp.where(rows == 0, ph[HALO - 2:HALO - 1, :],
                           jnp.where(rows == 1, ph[HALO - 1:HALO, :], pltpu.roll(p, 2, axis=0)))
            u = cw_ref[0:1, sl] * p2 + cw_ref[1:2, sl] * p1 + cw_ref[2:3, sl] * p
            cbu_ref[:, sl] = (cb_ref[:, sl].astype(F32) * u).astype(BF16)

    ya = jnp.dot(hm_ref[...], wa_ref[...], preferred_element_type=F32)
    yb = jnp.dot(cbu_ref[...], wb_ref[...], preferred_element_type=F32)
    merged = (jax.nn.sigmoid(ga_ref[...].astype(F32)) * ya
              + jax.nn.sigmoid(gb_ref[...].astype(F32)) * yb)
    out_ref[...] = merged.astype(out_ref.dtype)


def _merge(hm, proj, conv_w, wa, wb, *, s_len, col0, tm=512, tn=1024):
    t, width = hm.shape
    d_out = wa.shape[1]
    assert t % tm == 0 and s_len % tm == 0 and d_out % tn == 0 and col0 % width == 0
    assert width == d_out
    cblk = col0 // width
    nj = d_out // tn
    halo_blocks = tm // HALO
    kern = functools.partial(_merge_kernel, blocks_per_seq=s_len // tm, lane_chunk=256)
    halo_map = lambda off: (lambda i, j: (jnp.maximum(i * halo_blocks - 1, 0), cblk + off))
    return pl.pallas_call(
        kern,
        grid=(t // tm, nj),
        in_specs=[
            pl.BlockSpec((tm, width), lambda i, j: (i, 0)),
            pl.BlockSpec((tm, width), lambda i, j: (i, cblk)),
            pl.BlockSpec((tm, width), lambda i, j: (i, cblk + 1)),
            pl.BlockSpec((tm, width), lambda i, j: (i, cblk + 2)),
            pl.BlockSpec((HALO, width), halo_map(1)),
            pl.BlockSpec((HALO, width), halo_map(2)),
            pl.BlockSpec((CONV_K, width), lambda i, j: (0, 0)),
            pl.BlockSpec((tm, tn), lambda i, j: (i, (cblk + 3) * nj + j)),
            pl.BlockSpec((tm, tn), lambda i, j: (i, (cblk + 4) * nj + j)),
            pl.BlockSpec((width, tn), lambda i, j: (0, j)),
            pl.BlockSpec((width, tn), lambda i, j: (0, j)),
        ],
        out_specs=pl.BlockSpec((tm, tn), lambda i, j: (i, j)),
        out_shape=jax.ShapeDtypeStruct((t, d_out), BF16),
        scratch_shapes=[pltpu.VMEM((tm, width), BF16)],
        compiler_params=_params(("parallel", "arbitrary")),
        name="merge",
    )(hm, proj, proj, proj, proj, proj, conv_w, proj, proj, wa, wb)


def _outproj_kernel(m_ref, w_ref, x_ref, g_ref, x1_ref, hf_ref):
    x1 = x_ref[...] + jnp.dot(m_ref[...], w_ref[...], preferred_element_type=F32)
    x1_ref[...] = x1
    hf_ref[...] = (x1 * _rms_scale(x1) * g_ref[...]).astype(hf_ref.dtype)


def _outproj(merged, w_out, x2d, g, *, tm=512):
    t, d = x2d.shape
    assert t % tm == 0
    return pl.pallas_call(
        _outproj_kernel,
        grid=(t // tm,),
        in_specs=[
            pl.BlockSpec((tm, d), lambda i: (i, 0)),
            pl.BlockSpec((d, d), lambda i: (0, 0)),
            pl.BlockSpec((tm, d), lambda i: (i, 0)),
            pl.BlockSpec((1, d), lambda i: (0, 0)),
        ],
        out_specs=[
            pl.BlockSpec((tm, d), lambda i: (i, 0)),
            pl.BlockSpec((tm, d), lambda i: (i, 0)),
        ],
        out_shape=[
            jax.ShapeDtypeStruct((t, d), F32),
            jax.ShapeDtypeStruct((t, d), BF16),
        ],
        compiler_params=_params(("parallel",)),
        name="outproj",
    )(merged, w_out, x2d, g)


def _ffn_kernel(hf_ref, x1_ref, wg_ref, wu_ref, wd_ref, gf_ref, out_ref, acc_ref, *, final_norm):
    f = pl.program_id(1)
    hf = hf_ref[...]
    gate = jnp.dot(hf, wg_ref[...], preferred_element_type=F32)
    up = jnp.dot(hf, wu_ref[...], preferred_element_type=F32)
    act = (gate * jax.nn.sigmoid(gate) * up).astype(BF16)
    contrib = jnp.dot(act, wd_ref[...], preferred_element_type=F32)

    @pl.when(f == 0)
    def _():
        acc_ref[...] = contrib

    @pl.when(f > 0)
    def _():
        acc_ref[...] += contrib

    @pl.when(f == pl.num_programs(1) - 1)
    def _():
        x2 = x1_ref[...] + acc_ref[...]
        if final_norm:
            x2 = x2 * _rms_scale(x2) * gf_ref[...]
        out_ref[...] = x2


def _ffn(hf, x1, wg, wu, wd, gf, *, final_norm, tm=512, tf=512):
    t, d = x1.shape
    dff = wg.shape[1]
    assert t % tm == 0 and dff % tf == 0
    kern = functools.partial(_ffn_kernel, final_norm=final_norm)
    return pl.pallas_call(
        kern,
        grid=(t // tm, dff // tf),
        in_specs=[
            pl.BlockSpec((tm, d), lambda i, f: (i, 0)),
            pl.BlockSpec((tm, d), lambda i, f: (i, 0)),
            pl.BlockSpec((d, tf), lambda i, f: (0, f)),
            pl.BlockSpec((d, tf), lambda i, f: (0, f)),
            pl.BlockSpec((tf, d), lambda i, f: (f, 0)),
            pl.BlockSpec((1, d), lambda i, f: (0, 0)),
        ],
        out_specs=pl.BlockSpec((tm, d), lambda i, f: (i, 0)),
        out_shape=jax.ShapeDtypeStruct((t, d), F32),
        scratch_shapes=[pltpu.VMEM((tm, d), F32)],
        compiler_params=_params(("parallel", "arbitrary")),
        name="ffn",
    )(hf, x1, wg, wu, wd, gf)


def kernel(x, norm_mix, w_in, b_igate, b_fgate, conv_w, mh_norm, w_branch_a, w_branch_b,
           w_out, norm_ffn, w_gate, w_up, w_down, norm_final):
    bsz, s_len, d = x.shape
    depth = w_in.shape[0]
    v_width = mh_norm.shape[1]
    qk_width = v_width // 2
    conv_width = conv_w.shape[2]
    n_main = 2 * qk_width + 2 * v_width
    gate_end = n_main + 2 * HEADS
    assert w_in.shape[2] == gate_end + 3 * conv_width + 2 * d
    assert conv_w.shape[1] == CONV_K and b_igate.shape[1] == HEADS and conv_width == v_width
    q_scale = float((qk_width // HEADS) ** -0.5)

    x2d = x.reshape(bsz * s_len, d)
    for l in range(depth):
        w_l = w_in[l]
        w_main = jnp.concatenate([w_l[:, :n_main], w_l[:, gate_end:]], axis=1).astype(BF16)
        w_g = w_l[:, n_main:gate_end].astype(BF16)
        w_gate_pad = jnp.pad(w_g, ((0, 0), (0, GATE_LANES - 2 * HEADS)))
        bias = jnp.concatenate([b_igate[l], b_fgate[l]])
        bias_lane = jnp.pad(bias, (0, GATE_LANES - 2 * HEADS)).reshape(1, GATE_LANES)
        bias_sub = bias.reshape(2 * HEADS, 1)

        proj, gcol, grow = _inproj(x2d, norm_mix[l].reshape(1, d), w_main, w_gate_pad, w_g.T,
                                   qk_width=qk_width, q_scale=q_scale)
        hm = _mlstm(proj, gcol, grow, bias_lane, bias_sub, mh_norm[l].reshape(1, v_width),
                    bsz=bsz, s_len=s_len, qk_width=qk_width, v_width=v_width)
        merged = _merge(hm, proj, conv_w[l], w_branch_a[l].astype(BF16), w_branch_b[l].astype(BF16),
                        s_len=s_len, col0=n_main)
        x1, hf = _outproj(merged, w_out[l].astype(BF16), x2d, norm_ffn[l].reshape(1, d))
        x2d = _ffn(hf, x1, w_gate[l].astype(BF16), w_up[l].astype(BF16), w_down[l].astype(BF16),
                   norm_final.reshape(1, d), final_norm=(l == depth - 1))
    return x2d.reshape(bsz, s_len, d)
```

```python
import functools

import jax
import jax.numpy as jnp
from jax import lax
from jax.experimental import pallas as pl
from jax.experimental.pallas import tpu as pltpu

F32 = jnp.float32
BF16 = jnp.bfloat16

HEADS = 8
CHUNK = 128
CONV_K = 3
EPS = 1e-6
GATE_LANES = 128
VMEM_LIMIT_BYTES = 56 * 1024 * 1024


def _params(semantics):
    return pltpu.CompilerParams(dimension_semantics=semantics,
                                vmem_limit_bytes=VMEM_LIMIT_BYTES)


def _rms_scale(x):
    return lax.rsqrt(jnp.mean(x * x, axis=-1, keepdims=True) + EPS)


def _wprep_kernel(w_ref, main_ref, gate_ref, *, n_main, gate_end):
    main_ref[:, :n_main] = w_ref[:, :n_main].astype(BF16)
    main_ref[:, n_main:] = w_ref[:, gate_end:].astype(BF16)
    g = w_ref[:, n_main:n_main + GATE_LANES]
    lane = lax.broadcasted_iota(jnp.int32, g.shape, 1)
    gate_ref[...] = jnp.where(lane < gate_end - n_main, g, 0.0).astype(BF16)


def _wprep(w, *, n_main, gate_end, tk=128):
    d, n_in = w.shape
    n_out = n_in - (gate_end - n_main)
    assert d % tk == 0 and n_main % 128 == 0 and n_out % 128 == 0
    kern = functools.partial(_wprep_kernel, n_main=n_main, gate_end=gate_end)
    return pl.pallas_call(
        kern,
        grid=(d // tk,),
        in_specs=[pl.BlockSpec((tk, n_in), lambda i: (i, 0))],
        out_specs=[pl.BlockSpec((tk, n_out), lambda i: (i, 0)),
                   pl.BlockSpec((tk, GATE_LANES), lambda i: (i, 0))],
        out_shape=[jax.ShapeDtypeStruct((d, n_out), BF16),
                   jax.ShapeDtypeStruct((d, GATE_LANES), BF16)],
        compiler_params=_params(("parallel",)),
        name="wprep",
    )(w)


def _inproj_kernel(x_ref, g_ref, w_ref, wgate_ref, wgate_t_ref,
                   proj_ref, gcol_ref, grow_ref, h_ref, *, n_q_blocks, q_scale):
    j = pl.program_id(1)

    @pl.when(j == 0)
    def _():
        x = x_ref[...]
        h = (x * _rms_scale(x) * g_ref[...]).astype(BF16)
        h_ref[...] = h
        gcol_ref[...] = jnp.dot(h, wgate_ref[...], preferred_element_type=F32)
        grow_ref[...] = lax.dot_general(wgate_t_ref[...], h, (((1,), (1,)), ((), ())),
                                        preferred_element_type=F32)

    acc = jnp.dot(h_ref[...], w_ref[...], preferred_element_type=F32)
    scale = jnp.where(j < n_q_blocks, q_scale, 1.0).astype(F32)
    proj_ref[...] = (acc * scale).astype(BF16)


def _inproj(x2d, g, w_main, w_gate, w_gate_t, *, qk_width, q_scale, tm=1024, tn=1024):
    t, d = x2d.shape
    n = w_main.shape[1]
    assert t % tm == 0 and n % tn == 0 and qk_width % tn == 0
    kern = functools.partial(_inproj_kernel, n_q_blocks=qk_width // tn, q_scale=q_scale)
    return pl.pallas_call(
        kern,
        grid=(t // tm, n // tn),
        in_specs=[
            pl.BlockSpec((tm, d), lambda i, j: (i, 0)),
            pl.BlockSpec((1, d), lambda i, j: (0, 0)),
            pl.BlockSpec((d, tn), lambda i, j: (0, j)),
            pl.BlockSpec((d, GATE_LANES), lambda i, j: (0, 0)),
            pl.BlockSpec((2 * HEADS, d), lambda i, j: (0, 0)),
        ],
        out_specs=[
            pl.BlockSpec((tm, tn), lambda i, j: (i, j)),
            pl.BlockSpec((tm, GATE_LANES), lambda i, j: (i, 0)),
            pl.BlockSpec((2 * HEADS, tm), lambda i, j: (0, i)),
        ],
        out_shape=[
            jax.ShapeDtypeStruct((t, n), BF16),
            jax.ShapeDtypeStruct((t, GATE_LANES), F32),
            jax.ShapeDtypeStruct((2 * HEADS, t), F32),
        ],
        scratch_shapes=[pltpu.VMEM((tm, d), BF16)],
        compiler_params=_params(("parallel", "arbitrary")),
        name="inproj",
    )(x2d, g, w_main, w_gate, w_gate_t)


def _mlstm_kernel(q_ref, k_ref, v_ref, o_ref, gcol_ref, grow_ref, bias_lane_ref,
                  bias_sub_ref, mh_ref, out_ref, c_ref, n_ref, m_ref, *, dk, dv):
    L = CHUNK

    @pl.when(pl.program_id(1) == 0)
    def _():
        c_ref[...] = jnp.zeros_like(c_ref)
        n_ref[...] = jnp.zeros_like(n_ref)
        m_ref[...] = jnp.zeros_like(m_ref)

    row = lax.broadcasted_iota(jnp.int32, (L, L), 0)
    col = lax.broadcasted_iota(jnp.int32, (L, L), 1)
    causal = col <= row
    lower = causal.astype(F32)
    upper = (row <= col).astype(F32)

    xg_col = gcol_ref[...] + bias_lane_ref[...]
    cum_col = jnp.dot(lower, jax.nn.log_sigmoid(xg_col), precision=lax.Precision.HIGHEST,
                      preferred_element_type=F32)
    xg_row = grow_ref[...] + bias_sub_ref[...]
    cum_row = jnp.dot(jax.nn.log_sigmoid(xg_row[HEADS:, :]), upper,
                      precision=lax.Precision.HIGHEST, preferred_element_type=F32)

    for h in range(HEADS):
        q = q_ref[:, h * dk:(h + 1) * dk]
        k = k_ref[:, h * dk:(h + 1) * dk]
        v = v_ref[:, h * dv:(h + 1) * dv]
        li_row = xg_row[h:h + 1, :]
        b_row = cum_row[h:h + 1, :]
        li_col = xg_col[:, h:h + 1]
        b_col = cum_col[:, HEADS + h:HEADS + h + 1]
        g = cum_row[h:h + 1, L - 1:L]
        m_prev = m_ref[h, 0:1, 0:1]
        n_prev = n_ref[h, 0:1, :]
        c_prev = c_ref[h]

        dmat = jnp.where(causal, b_col - b_row + li_row, -jnp.inf)
        m_inter = b_col + m_prev
        m_comb = jnp.maximum(m_inter, jnp.max(dmat, axis=-1, keepdims=True))
        w = jnp.exp(dmat - m_comb)
        s = lax.dot_general(q, k, (((1,), (1,)), ((), ())), preferred_element_type=F32)
        scores = s * w
        inter = jnp.exp(m_inter - m_comb)
        qf = q.astype(F32)
        num = (jnp.dot(scores.astype(BF16), v, preferred_element_type=F32)
               + inter * jnp.dot(q, c_prev.astype(BF16), preferred_element_type=F32))
        den = (jnp.sum(scores, axis=-1, keepdims=True)
               + inter * jnp.sum(qf * n_prev, axis=-1, keepdims=True))
        h_out = num / jnp.maximum(jnp.abs(den), jnp.exp(-m_comb))

        a_col = g - b_col + li_col
        m_new = jnp.maximum(g + m_prev, jnp.max(a_col, axis=0, keepdims=True))
        decay = jnp.exp(g + m_prev - m_new)
        wk = jnp.exp(a_col - m_new)
        wv = (wk * v.astype(F32)).astype(BF16)
        c_ref[h] = decay * c_prev + lax.dot_general(
            k, wv, (((0,), (0,)), ((), ())), preferred_element_type=F32)
        n_new = decay * n_prev + jnp.sum(wk * k.astype(F32), axis=0, keepdims=True)
        n_ref[h] = jnp.broadcast_to(n_new, n_ref.shape[1:])
        m_ref[h] = jnp.broadcast_to(m_new, m_ref.shape[1:])

        hn = h_out * _rms_scale(h_out) * mh_ref[:, h * dv:(h + 1) * dv]
        gate = jax.nn.sigmoid(o_ref[:, h * dv:(h + 1) * dv].astype(F32))
        out_ref[:, h * dv:(h + 1) * dv] = (gate * hn).astype(out_ref.dtype)


def _mlstm(proj, gcol, grow, bias_lane, bias_sub, mh, *, bsz, s_len, qk_width, v_width):
    t = bsz * s_len
    nc = s_len // CHUNK
    dk, dv = qk_width // HEADS, v_width // HEADS
    assert v_width == 2 * qk_width and s_len % CHUNK == 0
    rowblk = lambda b, c: b * nc + c
    kern = functools.partial(_mlstm_kernel, dk=dk, dv=dv)
    return pl.pallas_call(
        kern,
        grid=(bsz, nc),
        in_specs=[
            pl.BlockSpec((CHUNK, qk_width), lambda b, c: (rowblk(b, c), 0)),
            pl.BlockSpec((CHUNK, qk_width), lambda b, c: (rowblk(b, c), 1)),
            pl.BlockSpec((CHUNK, v_width), lambda b, c: (rowblk(b, c), 1)),
            pl.BlockSpec((CHUNK, v_width), lambda b, c: (rowblk(b, c), 2)),
            pl.BlockSpec((CHUNK, GATE_LANES), lambda b, c: (rowblk(b, c), 0)),
            pl.BlockSpec((2 * HEADS, CHUNK), lambda b, c: (0, rowblk(b, c))),
            pl.BlockSpec((1, GATE_LANES), lambda b, c: (0, 0)),
            pl.BlockSpec((2 * HEADS, 1), lambda b, c: (0, 0)),
            pl.BlockSpec((1, v_width), lambda b, c: (0, 0)),
        ],
        out_specs=pl.BlockSpec((CHUNK, v_width), lambda b, c: (rowblk(b, c), 0)),
        out_shape=jax.ShapeDtypeStruct((t, v_width), BF16),
        scratch_shapes=[
            pltpu.VMEM((HEADS, dk, dv), F32),
            pltpu.VMEM((HEADS, 8, dk), F32),
            pltpu.VMEM((HEADS, 8, 128), F32),
        ],
        compiler_params=_params(("parallel", "arbitrary")),
        name="mlstm",
    )(proj, proj, proj, proj, gcol, grow, bias_lane, bias_sub, mh)


HALO = 16


def _merge_kernel(hm_ref, cb_ref, cc_ref, cx_ref, cc_halo_ref, cx_halo_ref, cw_ref,
                  ga_ref, gb_ref, wa_ref, wb_ref, out_ref, cbu_ref, *, blocks_per_seq, lane_chunk):
    i = pl.program_id(0)
    tm, width = cbu_ref.shape

    ya = jnp.dot(hm_ref[...], wa_ref[...], preferred_element_type=F32)

    keep_halo = (i % blocks_per_seq != 0).astype(F32)
    rows = lax.broadcasted_iota(jnp.int32, (tm, lane_chunk), 0)
    for c0 in range(0, width, lane_chunk):
        sl = slice(c0, c0 + lane_chunk)
        p = cc_ref[:, sl].astype(F32) * cx_ref[:, sl].astype(F32)
        ph = (cc_halo_ref[:, sl].astype(F32) * cx_halo_ref[:, sl].astype(F32)) * keep_halo
        p1 = jnp.where(rows == 0, ph[HALO - 1:HALO, :], pltpu.roll(p, 1, axis=0))
        p2 = jnp.where(rows == 0, ph[HALO - 2:HALO - 1, :],
                       jnp.where(rows == 1, ph[HALO - 1:HALO, :], pltpu.roll(p, 2, axis=0)))
        u = cw_ref[0:1, sl] * p2 + cw_ref[1:2, sl] * p1 + cw_ref[2:3, sl] * p
        cbu_ref[:, sl] = (cb_ref[:, sl].astype(F32) * u).astype(BF16)

    yb = jnp.dot(cbu_ref[...], wb_ref[...], preferred_element_type=F32)
    merged = (jax.nn.sigmoid(ga_ref[...].astype(F32)) * ya
              + jax.nn.sigmoid(gb_ref[...].astype(F32)) * yb)
    out_ref[...] = merged.astype(out_ref.dtype)


def _merge(hm, proj, conv_w, wa, wb, *, s_len, col0, tm=256):
    t, width = hm.shape
    d_out = wa.shape[1]
    assert t % tm == 0 and s_len % tm == 0 and col0 % width == 0 and width == d_out
    cblk = col0 // width
    halo_blocks = tm // HALO
    kern = functools.partial(_merge_kernel, blocks_per_seq=s_len // tm, lane_chunk=256)
    halo_map = lambda off: (lambda i: (jnp.maximum(i * halo_blocks - 1, 0), cblk + off))
    col_block = lambda c: pl.BlockSpec((tm, width), lambda i: (i, c))
    resident = lambda shape: pl.BlockSpec(shape, lambda i: (0, 0), pipeline_mode=pl.Buffered(1))
    return pl.pallas_call(
        kern,
        grid=(t // tm,),
        in_specs=[
            col_block(0),
            col_block(cblk),
            col_block(cblk + 1),
            col_block(cblk + 2),
            pl.BlockSpec((HALO, width), halo_map(1)),
            pl.BlockSpec((HALO, width), halo_map(2)),
            pl.BlockSpec((CONV_K, width), lambda i: (0, 0)),
            col_block(cblk + 3),
            col_block(cblk + 4),
            resident((width, d_out)),
            resident((width, d_out)),
        ],
        out_specs=pl.BlockSpec((tm, d_out), lambda i: (i, 0)),
        out_shape=jax.ShapeDtypeStruct((t, d_out), BF16),
        scratch_shapes=[pltpu.VMEM((tm, width), BF16)],
        compiler_params=_params(("parallel",)),
        name="merge",
    )(hm, proj, proj, proj, proj, proj, conv_w, proj, proj, wa, wb)


def _outproj_kernel(m_ref, w_ref, x_ref, g_ref, x1_ref, hf_ref):
    x1 = x_ref[...] + jnp.dot(m_ref[...], w_ref[...], preferred_element_type=F32)
    x1_ref[...] = x1
    hf_ref[...] = (x1 * _rms_scale(x1) * g_ref[...]).astype(hf_ref.dtype)


def _outproj(merged, w_out, x2d, g, *, tm=512):
    t, d = x2d.shape
    assert t % tm == 0
    return pl.pallas_call(
        _outproj_kernel,
        grid=(t // tm,),
        in_specs=[
            pl.BlockSpec((tm, d), lambda i: (i, 0)),
            pl.BlockSpec((d, d), lambda i: (0, 0)),
            pl.BlockSpec((tm, d), lambda i: (i, 0)),
            pl.BlockSpec((1, d), lambda i: (0, 0)),
        ],
        out_specs=[
            pl.BlockSpec((tm, d), lambda i: (i, 0)),
            pl.BlockSpec((tm, d), lambda i: (i, 0)),
        ],
        out_shape=[
            jax.ShapeDtypeStruct((t, d), F32),
            jax.ShapeDtypeStruct((t, d), BF16),
        ],
        compiler_params=_params(("parallel",)),
        name="outproj",
    )(merged, w_out, x2d, g)


def _ffn_kernel(hf_ref, x1_ref, wg_ref, wu_ref, wd_ref, gf_ref, out_ref, acc_ref, *, final_norm):
    f = pl.program_id(1)

    @pl.when(f == 0)
    def _():
        acc_ref[...] = jnp.zeros_like(acc_ref)

    hf = hf_ref[...]
    gate = jnp.dot(hf, wg_ref[...], preferred_element_type=F32)
    up = jnp.dot(hf, wu_ref[...], preferred_element_type=F32)
    act = (gate * jax.nn.sigmoid(gate) * up).astype(BF16)
    acc_ref[...] += jnp.dot(act, wd_ref[...], preferred_element_type=F32)

    @pl.when(f == pl.num_programs(1) - 1)
    def _():
        x2 = x1_ref[...] + acc_ref[...]
        if final_norm:
            x2 = x2 * _rms_scale(x2) * gf_ref[...]
        out_ref[...] = x2


def _ffn(hf, x1, wg, wu, wd, gf, *, final_norm, tm=512, tf=512):
    t, d = x1.shape
    dff = wg.shape[1]
    assert t % tm == 0 and dff % tf == 0
    kern = functools.partial(_ffn_kernel, final_norm=final_norm)
    return pl.pallas_call(
        kern,
        grid=(t // tm, dff // tf),
        in_specs=[
            pl.BlockSpec((tm, d), lambda i, f: (i, 0)),
            pl.BlockSpec((tm, d), lambda i, f: (i, 0)),
            pl.BlockSpec((d, tf), lambda i, f: (0, f)),
            pl.BlockSpec((d, tf), lambda i, f: (0, f)),
            pl.BlockSpec((tf, d), lambda i, f: (f, 0)),
            pl.BlockSpec((1, d), lambda i, f: (0, 0)),
        ],
        out_specs=pl.BlockSpec((tm, d), lambda i, f: (i, 0)),
        out_shape=jax.ShapeDtypeStruct((t, d), F32),
        scratch_shapes=[pltpu.VMEM((tm, d), F32)],
        compiler_params=_params(("parallel", "arbitrary")),
        name="ffn",
    )(hf, x1, wg, wu, wd, gf)


def kernel(x, norm_mix, w_in, b_igate, b_fgate, conv_w, mh_norm, w_branch_a, w_branch_b,
           w_out, norm_ffn, w_gate, w_up, w_down, norm_final):
    bsz, s_len, d = x.shape
    depth = w_in.shape[0]
    v_width = mh_norm.shape[1]
    qk_width = v_width // 2
    conv_width = conv_w.shape[2]
    n_main = 2 * qk_width + 2 * v_width
    gate_end = n_main + 2 * HEADS
    assert w_in.shape[2] == gate_end + 3 * conv_width + 2 * d
    assert conv_w.shape[1] == CONV_K and b_igate.shape[1] == HEADS and conv_width == v_width
    q_scale = float((qk_width // HEADS) ** -0.5)

    x2d = x.reshape(bsz * s_len, d)
    for l in range(depth):
        w_main, w_gate_pad = _wprep(w_in[l], n_main=n_main, gate_end=gate_end)
        w_g = w_gate_pad[:, :2 * HEADS]
        bias = jnp.concatenate([b_igate[l], b_fgate[l]])
        bias_lane = jnp.pad(bias, (0, GATE_LANES - 2 * HEADS)).reshape(1, GATE_LANES)
        bias_sub = bias.reshape(2 * HEADS, 1)

        proj, gcol, grow = _inproj(x2d, norm_mix[l].reshape(1, d), w_main, w_gate_pad, w_g.T,
                                   qk_width=qk_width, q_scale=q_scale)
        hm = _mlstm(proj, gcol, grow, bias_lane, bias_sub, mh_norm[l].reshape(1, v_width),
                    bsz=bsz, s_len=s_len, qk_width=qk_width, v_width=v_width)
        merged = _merge(hm, proj, conv_w[l], w_branch_a[l].astype(BF16), w_branch_b[l].astype(BF16),
                        s_len=s_len, col0=n_main)
        x1, hf = _outproj(merged, w_out[l].astype(BF16), x2d, norm_ffn[l].reshape(1, d))
        x2d = _ffn(hf, x1, w_gate[l].astype(BF16), w_up[l].astype(BF16), w_down[l].astype(BF16),
                   norm_final.reshape(1, d), final_norm=(l == depth - 1))
    return x2d.reshape(bsz, s_len, d)
```

```python
import functools

import jax
import jax.numpy as jnp
from jax import lax
from jax.experimental import pallas as pl
from jax.experimental.pallas import tpu as pltpu

F32 = jnp.float32
BF16 = jnp.bfloat16

HEADS = 8
CHUNK = 128
CONV_K = 3
EPS = 1e-6
LANES = 128
VMEM_LIMIT_BYTES = 56 * 1024 * 1024

NT_DIMS = (((1,), (1,)), ((), ()))
TN_DIMS = (((0,), (0,)), ((), ()))


def _params(semantics):
    return pltpu.CompilerParams(dimension_semantics=semantics,
                                vmem_limit_bytes=VMEM_LIMIT_BYTES)


def _rms_scale(x):
    return lax.rsqrt(jnp.mean(x * x, axis=-1, keepdims=True) + EPS)


def _inproj_kernel(x_ref, g_ref, wt_ref, wgate_ref, proj_ref, gcol_ref, h_ref, *, n_q_blocks, q_scale):
    j = pl.program_id(1)

    @pl.when(j == 0)
    def _():
        x = x_ref[...]
        h = (x * _rms_scale(x) * g_ref[...]).astype(BF16)
        h_ref[...] = h
        gcol_ref[...] = jnp.dot(h, wgate_ref[...], preferred_element_type=F32)

    acc = lax.dot_general(h_ref[...], wt_ref[...], NT_DIMS, preferred_element_type=F32)
    scale = jnp.where(j < n_q_blocks, q_scale, 1.0).astype(F32)
    proj_ref[...] = (acc * scale).astype(BF16)


def _inproj(x2d, g, w_main_t, w_gate, *, qk_width, q_scale, tm=1024, tn=1024):
    t, d = x2d.shape
    n = w_main_t.shape[0]
    assert t % tm == 0 and n % tn == 0 and qk_width % tn == 0
    kern = functools.partial(_inproj_kernel, n_q_blocks=qk_width // tn, q_scale=q_scale)
    return pl.pallas_call(
        kern,
        grid=(t // tm, n // tn),
        in_specs=[
            pl.BlockSpec((tm, d), lambda i, j: (i, 0)),
            pl.BlockSpec((1, d), lambda i, j: (0, 0)),
            pl.BlockSpec((tn, d), lambda i, j: (j, 0)),
            pl.BlockSpec((d, 2 * LANES), lambda i, j: (0, 0)),
        ],
        out_specs=[
            pl.BlockSpec((tm, tn), lambda i, j: (i, j)),
            pl.BlockSpec((tm, 2 * LANES), lambda i, j: (i, 0)),
        ],
        out_shape=[
            jax.ShapeDtypeStruct((t, n), BF16),
            jax.ShapeDtypeStruct((t, 2 * LANES), F32),
        ],
        scratch_shapes=[pltpu.VMEM((tm, d), BF16)],
        compiler_params=_params(("parallel", "arbitrary")),
        name="inproj",
    )(x2d, g, w_main_t, w_gate)


REP_D, REP_WK, REP_NM = range(3)


def _scan_rows(x, combine, identity):
    rows = lax.broadcasted_iota(jnp.int32, x.shape, 0)
    shift = 1
    while shift < x.shape[0]:
        x = combine(x, jnp.where(rows >= shift, pltpu.roll(x, shift, axis=0), identity))
        shift *= 2
    return x


def _mlstm_kernel(q_ref, k_ref, v_ref, o_ref, gcol_ref, bias_lane_ref, mh_ref, sel_ref,
                  out_ref, c_ref, n_ref, cn_ref, m_ref, s_ref, rep_ref, *, dk, dv):
    L = CHUNK

    @pl.when(pl.program_id(1) == 0)
    def _():
        c_ref[...] = jnp.zeros_like(c_ref)
        n_ref[...] = jnp.zeros_like(n_ref)
        cn_ref[...] = jnp.zeros_like(cn_ref)
        m_ref[...] = jnp.zeros_like(m_ref)

    heads = range(HEADS)
    blk = lambda h: slice(h * LANES, (h + 1) * LANES)
    qk_cols = lambda h: slice(h * dk, (h + 1) * dk)
    v_cols = lambda h: slice(h * dv, (h + 1) * dv)

    for h in heads:
        s_ref[h] = lax.dot_general(q_ref[:, qk_cols(h)], k_ref[:, qk_cols(h)], NT_DIMS,
                                   preferred_element_type=F32)

    row = lax.broadcasted_iota(jnp.int32, (L, L), 0)
    col = lax.broadcasted_iota(jnp.int32, (L, L), 1)
    causal = col <= row

    def replicate(slot, xcol):
        hi = xcol.astype(BF16)
        lo = (xcol - hi.astype(F32)).astype(BF16)
        rep_ref[slot] = jnp.dot(jnp.concatenate([hi, lo], axis=1), sel_ref[...],
                                preferred_element_type=F32)

    g2 = gcol_ref[...] + bias_lane_ref[...]
    b_c = _scan_rows(jax.nn.log_sigmoid(g2[:, LANES:]), jnp.add, 0.0)
    c_c = g2[:, :LANES] - b_c
    m_prev = m_ref[0:1, :]
    u_c = jnp.maximum(_scan_rows(c_c, jnp.maximum, -jnp.inf), m_prev)
    r = u_c[L - 1:L, :]
    m_ref[...] = jnp.broadcast_to(b_c[L - 1:L, :] + r, m_ref.shape)
    replicate(REP_D, r - u_c)
    replicate(REP_WK, c_c - r)
    replicate(REP_NM, -(b_c + u_c))
    crow = (c_c - r).T[:HEADS, :]
    mrow = jnp.broadcast_to(m_prev - r, (L, LANES)).T[:HEADS, :]
    decay = jnp.exp(mrow)

    ones = jnp.ones((L, LANES), BF16)
    for h in heads:
        d_h = rep_ref[REP_D, :, blk(h)]
        w = jnp.exp(jnp.where(causal, d_h + crow[h:h + 1, :], -jnp.inf))
        inter = jnp.exp(d_h + mrow[h:h + 1, :])
        lhs = jnp.concatenate([s_ref[h] * w, q_ref[:, qk_cols(h)].astype(F32) * inter], axis=1)
        rhs = jnp.concatenate([jnp.concatenate([v_ref[:, v_cols(h)], ones], axis=1), cn_ref[h]], axis=0)
        res = jnp.dot(lhs.astype(BF16), rhs, preferred_element_type=F32)
        num = res[:, :dv]
        inv = 1.0 / jnp.maximum(jnp.abs(res[:, dv:]), jnp.exp(rep_ref[REP_NM, :, blk(h)]))
        ms = jnp.mean(num * num, axis=-1, keepdims=True)
        rr = inv * lax.rsqrt(ms * inv * inv + EPS)
        gate = jax.nn.sigmoid(o_ref[:, v_cols(h)].astype(F32))
        hn = num * jnp.concatenate([rr, rr], axis=1) * mh_ref[:, v_cols(h)]
        out_ref[:, v_cols(h)] = (gate * hn).astype(out_ref.dtype)

    for h in heads:
        v = v_ref[:, v_cols(h)]
        wk = jnp.exp(rep_ref[REP_WK, :, blk(h)])
        wv = jnp.concatenate([wk, wk], axis=1) * v.astype(F32)
        upd = lax.dot_general(k_ref[:, qk_cols(h)], jnp.concatenate([wv, wk], axis=1).astype(BF16),
                              TN_DIMS, preferred_element_type=F32)
        dec = decay[h:h + 1, :]
        c_new = jnp.concatenate([dec, dec], axis=1) * c_ref[h] + upd[:, :dv]
        n_new = dec * n_ref[h] + upd[:, dv:]
        c_ref[h] = c_new
        n_ref[h] = n_new
        cn_ref[h] = jnp.concatenate([c_new, n_new], axis=1).astype(BF16)


def _mlstm(proj, gcol, bias_lane, mh, sel, *, bsz, s_len, qk_width, v_width):
    t = bsz * s_len
    nc = s_len // CHUNK
    dk, dv = qk_width // HEADS, v_width // HEADS
    assert v_width == 2 * qk_width and s_len % CHUNK == 0 and dk == CHUNK == LANES and dv == 2 * LANES
    rowblk = lambda b, c: b * nc + c
    const = lambda shape: pl.BlockSpec(shape, lambda b, c: (0, 0))
    kern = functools.partial(_mlstm_kernel, dk=dk, dv=dv)
    return pl.pallas_call(
        kern,
        grid=(bsz, nc),
        in_specs=[
            pl.BlockSpec((CHUNK, qk_width), lambda b, c: (rowblk(b, c), 0)),
            pl.BlockSpec((CHUNK, qk_width), lambda b, c: (rowblk(b, c), 1)),
            pl.BlockSpec((CHUNK, v_width), lambda b, c: (rowblk(b, c), 1)),
            pl.BlockSpec((CHUNK, v_width), lambda b, c: (rowblk(b, c), 2)),
            pl.BlockSpec((CHUNK, 2 * LANES), lambda b, c: (rowblk(b, c), 0)),
            const((1, 2 * LANES)),
            const((1, v_width)),
            const((2 * LANES, HEADS * LANES)),
        ],
        out_specs=pl.BlockSpec((CHUNK, v_width), lambda b, c: (rowblk(b, c), 0)),
        out_shape=jax.ShapeDtypeStruct((t, v_width), BF16),
        scratch_shapes=[
            pltpu.VMEM((HEADS, dk, dv), F32),
            pltpu.VMEM((HEADS, dk, LANES), F32),
            pltpu.VMEM((HEADS, dk, dv + LANES), BF16),
            pltpu.VMEM((8, LANES), F32),
            pltpu.VMEM((HEADS, CHUNK, CHUNK), F32),
            pltpu.VMEM((3, CHUNK, HEADS * LANES), F32),
        ],
        compiler_params=_params(("parallel", "arbitrary")),
        name="mlstm",
    )(proj, proj, proj, proj, gcol, bias_lane, mh, sel)


HALO = 16


def _merge_kernel(hm_ref, cb_ref, cc_ref, cx_ref, cc_halo_ref, cx_halo_ref, cw_ref,
                  ga_ref, gb_ref, wa_ref, wb_ref, out_ref, cbu_ref, *, blocks_per_seq, lane_chunk):
    i = pl.program_id(0)
    tm, width = cbu_ref.shape

    ya = jnp.dot(hm_ref[...], wa_ref[...], preferred_element_type=F32)

    keep_halo = (i % blocks_per_seq != 0).astype(F32)
    rows = lax.broadcasted_iota(jnp.int32, (tm, lane_chunk), 0)
    for c0 in range(0, width, lane_chunk):
        sl = slice(c0, c0 + lane_chunk)
        p = cc_ref[:, sl].astype(F32) * cx_ref[:, sl].astype(F32)
        ph = (cc_halo_ref[:, sl].astype(F32) * cx_halo_ref[:, sl].astype(F32)) * keep_halo
        p1 = jnp.where(rows == 0, ph[HALO - 1:HALO, :], pltpu.roll(p, 1, axis=0))
        p2 = jnp.where(rows == 0, ph[HALO - 2:HALO - 1, :],
                       jnp.where(rows == 1, ph[HALO - 1:HALO, :], pltpu.roll(p, 2, axis=0)))
        u = cw_ref[0:1, sl] * p2 + cw_ref[1:2, sl] * p1 + cw_ref[2:3, sl] * p
        cbu_ref[:, sl] = (cb_ref[:, sl].astype(F32) * u).astype(BF16)

    yb = jnp.dot(cbu_ref[...], wb_ref[...], preferred_element_type=F32)
    merged = (jax.nn.sigmoid(ga_ref[...].astype(F32)) * ya
              + jax.nn.sigmoid(gb_ref[...].astype(F32)) * yb)
    out_ref[...] = merged.astype(out_ref.dtype)


def _merge(hm, proj, conv_w, wa, wb, *, s_len, col0, tm=256):
    t, width = hm.shape
    d_out = wa.shape[1]
    assert t % tm == 0 and s_len % tm == 0 and col0 % width == 0 and width == d_out
    cblk = col0 // width
    halo_blocks = tm // HALO
    kern = functools.partial(_merge_kernel, blocks_per_seq=s_len // tm, lane_chunk=256)
    halo_map = lambda off: (lambda i: (jnp.maximum(i * halo_blocks - 1, 0), cblk + off))
    col_block = lambda c: pl.BlockSpec((tm, width), lambda i: (i, c))
    resident = lambda shape: pl.BlockSpec(shape, lambda i: (0, 0), pipeline_mode=pl.Buffered(1))
    return pl.pallas_call(
        kern,
        grid=(t // tm,),
        in_specs=[
            col_block(0),
            col_block(cblk),
            col_block(cblk + 1),
            col_block(cblk + 2),
            pl.BlockSpec((HALO, width), halo_map(1)),
            pl.BlockSpec((HALO, width), halo_map(2)),
            pl.BlockSpec((CONV_K, width), lambda i: (0, 0)),
            col_block(cblk + 3),
            col_block(cblk + 4),
            resident((width, d_out)),
            resident((width, d_out)),
        ],
        out_specs=pl.BlockSpec((tm, d_out), lambda i: (i, 0)),
        out_shape=jax.ShapeDtypeStruct((t, d_out), BF16),
        scratch_shapes=[pltpu.VMEM((tm, width), BF16)],
        compiler_params=_params(("parallel",)),
        name="merge",
    )(hm, proj, proj, proj, proj, proj, conv_w, proj, proj, wa, wb)


def _outproj_kernel(m_ref, w_ref, x_ref, g_ref, x1_ref, hf_ref):
    x1 = x_ref[...] + jnp.dot(m_ref[...], w_ref[...], preferred_element_type=F32)
    x1_ref[...] = x1
    hf_ref[...] = (x1 * _rms_scale(x1) * g_ref[...]).astype(hf_ref.dtype)


def _outproj(merged, w_out, x2d, g, *, tm=512):
    t, d = x2d.shape
    assert t % tm == 0
    return pl.pallas_call(
        _outproj_kernel,
        grid=(t // tm,),
        in_specs=[
            pl.BlockSpec((tm, d), lambda i: (i, 0)),
            pl.BlockSpec((d, d), lambda i: (0, 0)),
            pl.BlockSpec((tm, d), lambda i: (i, 0)),
            pl.BlockSpec((1, d), lambda i: (0, 0)),
        ],
        out_specs=[
            pl.BlockSpec((tm, d), lambda i: (i, 0)),
            pl.BlockSpec((tm, d), lambda i: (i, 0)),
        ],
        out_shape=[
            jax.ShapeDtypeStruct((t, d), F32),
            jax.ShapeDtypeStruct((t, d), BF16),
        ],
        compiler_params=_params(("parallel",)),
        name="outproj",
    )(merged, w_out, x2d, g)


def _ffn_kernel(hf_ref, x1_ref, wg_ref, wu_ref, wd_ref, gf_ref, out_ref, acc_ref, *, final_norm):
    f = pl.program_id(1)

    @pl.when(f == 0)
    def _():
        acc_ref[...] = jnp.zeros_like(acc_ref)

    hf = hf_ref[...]
    gate = jnp.dot(hf, wg_ref[...], preferred_element_type=F32)
    up = jnp.dot(hf, wu_ref[...], preferred_element_type=F32)
    act = (gate * jax.nn.sigmoid(gate) * up).astype(BF16)
    acc_ref[...] += jnp.dot(act, wd_ref[...], preferred_element_type=F32)

    @pl.when(f == pl.num_programs(1) - 1)
    def _():
        x2 = x1_ref[...] + acc_ref[...]
        if final_norm:
            x2 = x2 * _rms_scale(x2) * gf_ref[...]
        out_ref[...] = x2


def _ffn(hf, x1, wg, wu, wd, gf, *, final_norm, tm=512, tf=512):
    t, d = x1.shape
    dff = wg.shape[1]
    assert t % tm == 0 and dff % tf == 0
    kern = functools.partial(_ffn_kernel, final_norm=final_norm)
    return pl.pallas_call(
        kern,
        grid=(t // tm, dff // tf),
        in_specs=[
            pl.BlockSpec((tm, d), lambda i, f: (i, 0)),
            pl.BlockSpec((tm, d), lambda i, f: (i, 0)),
            pl.BlockSpec((d, tf), lambda i, f: (0, f)),
            pl.BlockSpec((d, tf), lambda i, f: (0, f)),
            pl.BlockSpec((tf, d), lambda i, f: (f, 0)),
            pl.BlockSpec((1, d), lambda i, f: (0, 0)),
        ],
        out_specs=pl.BlockSpec((tm, d), lambda i, f: (i, 0)),
        out_shape=jax.ShapeDtypeStruct((t, d), F32),
        scratch_shapes=[pltpu.VMEM((tm, d), F32)],
        compiler_params=_params(("parallel", "arbitrary")),
        name="ffn",
    )(hf, x1, wg, wu, wd, gf)


def kernel(x, norm_mix, w_in, b_igate, b_fgate, conv_w, mh_norm, w_branch_a, w_branch_b,
           w_out, norm_ffn, w_gate, w_up, w_down, norm_final):
    bsz, s_len, d = x.shape
    depth = w_in.shape[0]
    v_width = mh_norm.shape[1]
    qk_width = v_width // 2
    conv_width = conv_w.shape[2]
    n_main = 2 * qk_width + 2 * v_width
    gate_end = n_main + 2 * HEADS
    assert w_in.shape[2] == gate_end + 3 * conv_width + 2 * d
    assert conv_w.shape[1] == CONV_K and b_igate.shape[1] == HEADS and conv_width == v_width
    q_scale = float((qk_width // HEADS) ** -0.5)

    lane = jnp.arange(2 * LANES)[:, None] % LANES
    sel = (lane == jnp.arange(HEADS * LANES)[None, :] // LANES).astype(BF16)
    pad_lanes = lambda a: jnp.pad(a, ((0, 0), (0, LANES - a.shape[1])))

    x2d = x.reshape(bsz * s_len, d)
    for l in range(depth):
        w_t = jnp.swapaxes(w_in[l], 0, 1)
        w_main_t = jnp.concatenate([w_t[:n_main], w_t[gate_end:]], axis=0).astype(BF16)
        w_gate_t = w_t[n_main:gate_end].astype(BF16)
        w_gate_col = jnp.concatenate([pad_lanes(w_gate_t[:HEADS].T), pad_lanes(w_gate_t[HEADS:].T)], axis=1)
        bias_lane = jnp.concatenate([pad_lanes(b_igate[l][None, :]), pad_lanes(b_fgate[l][None, :])], axis=1)

        proj, gcol = _inproj(x2d, norm_mix[l].reshape(1, d), w_main_t, w_gate_col,
                             qk_width=qk_width, q_scale=q_scale)
        hm = _mlstm(proj, gcol, bias_lane, mh_norm[l].reshape(1, v_width), sel,
                    bsz=bsz, s_len=s_len, qk_width=qk_width, v_width=v_width)
        merged = _merge(hm, proj, conv_w[l], w_branch_a[l].astype(BF16), w_branch_b[l].astype(BF16),
                        s_len=s_len, col0=n_main)
        x1, hf = _outproj(merged, w_out[l].astype(BF16), x2d, norm_ffn[l].reshape(1, d))
        x2d = _ffn(hf, x1, w_gate[l].astype(BF16), w_up[l].astype(BF16), w_down[l].astype(BF16),
                   norm_final.reshape(1, d), final_norm=(l == depth - 1))
    return x2d.reshape(bsz, s_len, d)
```

```python
import functools

import jax
import jax.numpy as jnp
from jax import lax
from jax.experimental import pallas as pl
from jax.experimental.pallas import tpu as pltpu

F32 = jnp.float32
BF16 = jnp.bfloat16

HEADS = 8
CHUNK = 128
CONV_K = 3
EPS = 1e-6
LANES = 128
VMEM_LIMIT_BYTES = 56 * 1024 * 1024

NT_DIMS = (((1,), (1,)), ((), ()))
TN_DIMS = (((0,), (0,)), ((), ()))


def _params(semantics):
    return pltpu.CompilerParams(dimension_semantics=semantics,
                                vmem_limit_bytes=VMEM_LIMIT_BYTES)


def _rms_scale(x):
    return lax.rsqrt(jnp.mean(x * x, axis=-1, keepdims=True) + EPS)


def _wprep_kernel(wt_ref, main_ref, gate_ref, *, n_main, gate_end):
    main_ref[:n_main, :] = wt_ref[:n_main, :].astype(BF16)
    main_ref[n_main:, :] = wt_ref[gate_end:, :].astype(BF16)
    gate_ref[...] = wt_ref[n_main:gate_end, :].astype(BF16)


def _wprep(w_t, *, n_main, gate_end):
    n_in, d = w_t.shape
    n_gate = gate_end - n_main
    assert n_main % 16 == 0 and n_gate % 16 == 0 and d % LANES == 0
    kern = functools.partial(_wprep_kernel, n_main=n_main, gate_end=gate_end)
    return pl.pallas_call(
        kern,
        grid=(d // LANES,),
        in_specs=[pl.BlockSpec((n_in, LANES), lambda i: (0, i))],
        out_specs=[pl.BlockSpec((n_in - n_gate, LANES), lambda i: (0, i)),
                   pl.BlockSpec((n_gate, LANES), lambda i: (0, i))],
        out_shape=[jax.ShapeDtypeStruct((n_in - n_gate, d), BF16),
                   jax.ShapeDtypeStruct((n_gate, d), BF16)],
        compiler_params=_params(("parallel",)),
        name="wprep",
    )(w_t)


def _inproj_kernel(x_ref, g_ref, wt_ref, wgate_ref, proj_ref, gcol_ref, h_ref, *, n_q_blocks, q_scale):
    j = pl.program_id(1)

    @pl.when(j == 0)
    def _():
        x = x_ref[...]
        h = (x * _rms_scale(x) * g_ref[...]).astype(BF16)
        h_ref[...] = h
        gcol_ref[...] = jnp.dot(h, wgate_ref[...], preferred_element_type=F32)

    acc = lax.dot_general(h_ref[...], wt_ref[...], NT_DIMS, preferred_element_type=F32)
    scale = jnp.where(j < n_q_blocks, q_scale, 1.0).astype(F32)
    proj_ref[...] = (acc * scale).astype(BF16)


def _inproj(x2d, g, w_main_t, w_gate, *, qk_width, q_scale, tm=1024, tn=1024):
    t, d = x2d.shape
    n = w_main_t.shape[0]
    assert t % tm == 0 and n % tn == 0 and qk_width % tn == 0
    kern = functools.partial(_inproj_kernel, n_q_blocks=qk_width // tn, q_scale=q_scale)
    return pl.pallas_call(
        kern,
        grid=(t // tm, n // tn),
        in_specs=[
            pl.BlockSpec((tm, d), lambda i, j: (i, 0)),
            pl.BlockSpec((1, d), lambda i, j: (0, 0)),
            pl.BlockSpec((tn, d), lambda i, j: (j, 0)),
            pl.BlockSpec((d, 2 * LANES), lambda i, j: (0, 0)),
        ],
        out_specs=[
            pl.BlockSpec((tm, tn), lambda i, j: (i, j)),
            pl.BlockSpec((tm, 2 * LANES), lambda i, j: (i, 0)),
        ],
        out_shape=[
            jax.ShapeDtypeStruct((t, n), BF16),
            jax.ShapeDtypeStruct((t, 2 * LANES), F32),
        ],
        scratch_shapes=[pltpu.VMEM((tm, d), BF16)],
        compiler_params=_params(("parallel", "arbitrary")),
        name="inproj",
    )(x2d, g, w_main_t, w_gate)


REP_D, REP_WK, REP_NM = range(3)


def _scan_rows(x, combine, identity):
    rows = lax.broadcasted_iota(jnp.int32, x.shape, 0)
    shift = 1
    while shift < x.shape[0]:
        x = combine(x, jnp.where(rows >= shift, pltpu.roll(x, shift, axis=0), identity))
        shift *= 2
    return x


def _mlstm_kernel(q_ref, k_ref, v_ref, o_ref, gcol_ref, bias_lane_ref, mh_ref, sel_ref,
                  out_ref, c_ref, n_ref, cn_ref, m_ref, s_ref, rep_ref, *, dk, dv):
    L = CHUNK

    @pl.when(pl.program_id(1) == 0)
    def _():
        c_ref[...] = jnp.zeros_like(c_ref)
        n_ref[...] = jnp.zeros_like(n_ref)
        cn_ref[...] = jnp.zeros_like(cn_ref)
        m_ref[...] = jnp.zeros_like(m_ref)

    heads = range(HEADS)
    blk = lambda h: slice(h * LANES, (h + 1) * LANES)
    qk_cols = lambda h: slice(h * dk, (h + 1) * dk)
    v_cols = lambda h: slice(h * dv, (h + 1) * dv)

    for h in heads:
        s_ref[h] = lax.dot_general(q_ref[:, qk_cols(h)], k_ref[:, qk_cols(h)], NT_DIMS,
                                   preferred_element_type=F32)

    row = lax.broadcasted_iota(jnp.int32, (L, L), 0)
    col = lax.broadcasted_iota(jnp.int32, (L, L), 1)
    causal = col <= row

    def replicate(slot, xcol):
        hi = xcol.astype(BF16)
        lo = (xcol - hi.astype(F32)).astype(BF16)
        rep_ref[slot] = jnp.dot(jnp.concatenate([hi, lo], axis=1), sel_ref[...],
                                preferred_element_type=F32)

    g2 = gcol_ref[...] + bias_lane_ref[...]
    b_c = _scan_rows(jax.nn.log_sigmoid(g2[:, LANES:]), jnp.add, 0.0)
    c_c = g2[:, :LANES] - b_c
    m_prev = m_ref[0:1, :]
    u_c = jnp.maximum(_scan_rows(c_c, jnp.maximum, -jnp.inf), m_prev)
    r = u_c[L - 1:L, :]
    m_ref[...] = jnp.broadcast_to(b_c[L - 1:L, :] + r, m_ref.shape)
    replicate(REP_D, r - u_c)
    replicate(REP_WK, c_c - r)
    replicate(REP_NM, -(b_c + u_c))
    crow = (c_c - r).T[:HEADS, :]
    mrow = jnp.broadcast_to(m_prev - r, (L, LANES)).T[:HEADS, :]
    decay = jnp.exp(mrow)

    ones = jnp.ones((L, LANES), BF16)
    for h in heads:
        d_h = rep_ref[REP_D, :, blk(h)]
        w = jnp.exp(jnp.where(causal, d_h + crow[h:h + 1, :], -jnp.inf))
        inter = jnp.exp(d_h + mrow[h:h + 1, :])
        lhs = jnp.concatenate([s_ref[h] * w, q_ref[:, qk_cols(h)].astype(F32) * inter], axis=1)
        rhs = jnp.concatenate([jnp.concatenate([v_ref[:, v_cols(h)], ones], axis=1), cn_ref[h]], axis=0)
        res = jnp.dot(lhs.astype(BF16), rhs, preferred_element_type=F32)
        num = res[:, :dv]
        inv = 1.0 / jnp.maximum(jnp.abs(res[:, dv:]), jnp.exp(rep_ref[REP_NM, :, blk(h)]))
        ms = jnp.mean(num * num, axis=-1, keepdims=True)
        rr = inv * lax.rsqrt(ms * inv * inv + EPS)
        gate = jax.nn.sigmoid(o_ref[:, v_cols(h)].astype(F32))
        hn = num * jnp.concatenate([rr, rr], axis=1) * mh_ref[:, v_cols(h)]
        out_ref[:, v_cols(h)] = (gate * hn).astype(out_ref.dtype)

    for h in heads:
        v = v_ref[:, v_cols(h)]
        wk = jnp.exp(rep_ref[REP_WK, :, blk(h)])
        wv = jnp.concatenate([wk, wk], axis=1) * v.astype(F32)
        upd = lax.dot_general(k_ref[:, qk_cols(h)], jnp.concatenate([wv, wk], axis=1).astype(BF16),
                              TN_DIMS, preferred_element_type=F32)
        dec = decay[h:h + 1, :]
        c_new = jnp.concatenate([dec, dec], axis=1) * c_ref[h] + upd[:, :dv]
        n_new = dec * n_ref[h] + upd[:, dv:]
        c_ref[h] = c_new
        n_ref[h] = n_new
        cn_ref[h] = jnp.concatenate([c_new, n_new], axis=1).astype(BF16)


def _mlstm(proj, gcol, bias_lane, mh, sel, *, bsz, s_len, qk_width, v_width):
    t = bsz * s_len
    nc = s_len // CHUNK
    dk, dv = qk_width // HEADS, v_width // HEADS
    assert v_width == 2 * qk_width and s_len % CHUNK == 0 and dk == CHUNK == LANES and dv == 2 * LANES
    rowblk = lambda b, c: b * nc + c
    const = lambda shape: pl.BlockSpec(shape, lambda b, c: (0, 0))
    kern = functools.partial(_mlstm_kernel, dk=dk, dv=dv)
    return pl.pallas_call(
        kern,
        grid=(bsz, nc),
        in_specs=[
            pl.BlockSpec((CHUNK, qk_width), lambda b, c: (rowblk(b, c), 0)),
            pl.BlockSpec((CHUNK, qk_width), lambda b, c: (rowblk(b, c), 1)),
            pl.BlockSpec((CHUNK, v_width), lambda b, c: (rowblk(b, c), 1)),
            pl.BlockSpec((CHUNK, v_width), lambda b, c: (rowblk(b, c), 2)),
            pl.BlockSpec((CHUNK, 2 * LANES), lambda b, c: (rowblk(b, c), 0)),
            const((1, 2 * LANES)),
            const((1, v_width)),
            const((2 * LANES, HEADS * LANES)),
        ],
        out_specs=pl.BlockSpec((CHUNK, v_width), lambda b, c: (rowblk(b, c), 0)),
        out_shape=jax.ShapeDtypeStruct((t, v_width), BF16),
        scratch_shapes=[
            pltpu.VMEM((HEADS, dk, dv), F32),
            pltpu.VMEM((HEADS, dk, LANES), F32),
            pltpu.VMEM((HEADS, dk, dv + LANES), BF16),
            pltpu.VMEM((8, LANES), F32),
            pltpu.VMEM((HEADS, CHUNK, CHUNK), F32),
            pltpu.VMEM((3, CHUNK, HEADS * LANES), F32),
        ],
        compiler_params=_params(("parallel", "arbitrary")),
        name="mlstm",
    )(proj, proj, proj, proj, gcol, bias_lane, mh, sel)


HALO = 16


def _mix_kernel(hm_ref, cb_ref, cc_ref, cx_ref, cc_halo_ref, cx_halo_ref, cw_ref, ga_ref, gb_ref,
                wa_ref, wb_ref, wo_ref, x_ref, g_ref, x1_ref, hf_ref, cbu_ref, *, blocks_per_seq, lane_chunk):
    i = pl.program_id(0)
    tm, width = cbu_ref.shape

    ya = jnp.dot(hm_ref[...], wa_ref[...], preferred_element_type=F32)

    keep_halo = (i % blocks_per_seq != 0).astype(F32)
    rows = lax.broadcasted_iota(jnp.int32, (tm, lane_chunk), 0)
    for c0 in range(0, width, lane_chunk):
        sl = slice(c0, c0 + lane_chunk)
        p = cc_ref[:, sl].astype(F32) * cx_ref[:, sl].astype(F32)
        ph = (cc_halo_ref[:, sl].astype(F32) * cx_halo_ref[:, sl].astype(F32)) * keep_halo
        p1 = jnp.where(rows == 0, ph[HALO - 1:HALO, :], pltpu.roll(p, 1, axis=0))
        p2 = jnp.where(rows == 0, ph[HALO - 2:HALO - 1, :],
                       jnp.where(rows == 1, ph[HALO - 1:HALO, :], pltpu.roll(p, 2, axis=0)))
        u = cw_ref[0:1, sl] * p2 + cw_ref[1:2, sl] * p1 + cw_ref[2:3, sl] * p
        cbu_ref[:, sl] = (cb_ref[:, sl].astype(F32) * u).astype(BF16)

    yb = jnp.dot(cbu_ref[...], wb_ref[...], preferred_element_type=F32)
    merged = (jax.nn.sigmoid(ga_ref[...].astype(F32)) * ya
              + jax.nn.sigmoid(gb_ref[...].astype(F32)) * yb)
    x1 = x_ref[...] + jnp.dot(merged.astype(BF16), wo_ref[...], preferred_element_type=F32)
    x1_ref[...] = x1
    hf_ref[...] = (x1 * _rms_scale(x1) * g_ref[...]).astype(hf_ref.dtype)


def _mix(hm, proj, conv_w, wa, wb, wo, x2d, g, *, s_len, col0, tm=256):
    t, width = hm.shape
    d = x2d.shape[1]
    assert t % tm == 0 and s_len % tm == 0 and col0 % width == 0 and width == d
    assert wa.shape == wb.shape == wo.shape == (width, d)
    cblk = col0 // width
    halo_blocks = tm // HALO
    kern = functools.partial(_mix_kernel, blocks_per_seq=s_len // tm, lane_chunk=256)
    halo_map = lambda off: (lambda i: (jnp.maximum(i * halo_blocks - 1, 0), cblk + off))
    col_block = lambda c: pl.BlockSpec((tm, width), lambda i: (i, c))
    resident = lambda shape: pl.BlockSpec(shape, lambda i: (0, 0), pipeline_mode=pl.Buffered(1))
    return pl.pallas_call(
        kern,
        grid=(t // tm,),
        in_specs=[
            col_block(0),
            col_block(cblk),
            col_block(cblk + 1),
            col_block(cblk + 2),
            pl.BlockSpec((HALO, width), halo_map(1)),
            pl.BlockSpec((HALO, width), halo_map(2)),
            pl.BlockSpec((CONV_K, width), lambda i: (0, 0)),
            col_block(cblk + 3),
            col_block(cblk + 4),
            resident((width, d)),
            resident((width, d)),
            resident((width, d)),
            col_block(0),
            pl.BlockSpec((1, d), lambda i: (0, 0)),
        ],
        out_specs=[col_block(0), col_block(0)],
        out_shape=[jax.ShapeDtypeStruct((t, d), F32), jax.ShapeDtypeStruct((t, d), BF16)],
        scratch_shapes=[pltpu.VMEM((tm, width), BF16)],
        compiler_params=_params(("parallel",)),
        name="mix",
    )(hm, proj, proj, proj, proj, proj, conv_w, proj, proj, wa, wb, wo, x2d, g)


def _ffn_kernel(hf_ref, x1_ref, wg_ref, wu_ref, wd_ref, gf_ref, out_ref, acc_ref, *, final_norm):
    f = pl.program_id(1)

    @pl.when(f == 0)
    def _():
        acc_ref[...] = jnp.zeros_like(acc_ref)

    hf = hf_ref[...]
    gate = jnp.dot(hf, wg_ref[...], preferred_element_type=F32)
    up = jnp.dot(hf, wu_ref[...], preferred_element_type=F32)
    act = (gate * jax.nn.sigmoid(gate) * up).astype(BF16)
    acc_ref[...] += jnp.dot(act, wd_ref[...], preferred_element_type=F32)

    @pl.when(f == pl.num_programs(1) - 1)
    def _():
        x2 = x1_ref[...] + acc_ref[...]
        if final_norm:
            x2 = x2 * _rms_scale(x2) * gf_ref[...]
        out_ref[...] = x2


def _ffn(hf, x1, wg, wu, wd, gf, *, final_norm, tm=512, tf=512):
    t, d = x1.shape
    dff = wg.shape[1]
    assert t % tm == 0 and dff % tf == 0
    kern = functools.partial(_ffn_kernel, final_norm=final_norm)
    return pl.pallas_call(
        kern,
        grid=(t // tm, dff // tf),
        in_specs=[
            pl.BlockSpec((tm, d), lambda i, f: (i, 0)),
            pl.BlockSpec((tm, d), lambda i, f: (i, 0)),
            pl.BlockSpec((d, tf), lambda i, f: (0, f)),
            pl.BlockSpec((d, tf), lambda i, f: (0, f)),
            pl.BlockSpec((tf, d), lambda i, f: (f, 0)),
            pl.BlockSpec((1, d), lambda i, f: (0, 0)),
        ],
        out_specs=pl.BlockSpec((tm, d), lambda i, f: (i, 0)),
        out_shape=jax.ShapeDtypeStruct((t, d), F32),
        scratch_shapes=[pltpu.VMEM((tm, d), F32)],
        compiler_params=_params(("parallel", "arbitrary")),
        name="ffn",
    )(hf, x1, wg, wu, wd, gf)


def kernel(x, norm_mix, w_in, b_igate, b_fgate, conv_w, mh_norm, w_branch_a, w_branch_b,
           w_out, norm_ffn, w_gate, w_up, w_down, norm_final):
    bsz, s_len, d = x.shape
    depth = w_in.shape[0]
    v_width = mh_norm.shape[1]
    qk_width = v_width // 2
    conv_width = conv_w.shape[2]
    n_main = 2 * qk_width + 2 * v_width
    gate_end = n_main + 2 * HEADS
    assert w_in.shape[2] == gate_end + 3 * conv_width + 2 * d
    assert conv_w.shape[1] == CONV_K and b_igate.shape[1] == HEADS and conv_width == v_width
    q_scale = float((qk_width // HEADS) ** -0.5)

    lane = jnp.arange(2 * LANES)[:, None] % LANES
    sel = (lane == jnp.arange(HEADS * LANES)[None, :] // LANES).astype(BF16)
    pad_lanes = lambda a: jnp.pad(a, ((0, 0), (0, LANES - a.shape[1])))

    x2d = x.reshape(bsz * s_len, d)
    for l in range(depth):
        w_main_t, w_gate_t = _wprep(jnp.swapaxes(w_in[l], 0, 1), n_main=n_main, gate_end=gate_end)
        w_gate_col = jnp.concatenate([pad_lanes(w_gate_t[:HEADS].T), pad_lanes(w_gate_t[HEADS:].T)], axis=1)
        bias_lane = jnp.concatenate([pad_lanes(b_igate[l][None, :]), pad_lanes(b_fgate[l][None, :])], axis=1)

        proj, gcol = _inproj(x2d, norm_mix[l].reshape(1, d), w_main_t, w_gate_col,
                             qk_width=qk_width, q_scale=q_scale)
        hm = _mlstm(proj, gcol, bias_lane, mh_norm[l].reshape(1, v_width), sel,
                    bsz=bsz, s_len=s_len, qk_width=qk_width, v_width=v_width)
        x1, hf = _mix(hm, proj, conv_w[l], w_branch_a[l].astype(BF16), w_branch_b[l].astype(BF16),
                      w_out[l].astype(BF16), x2d, norm_ffn[l].reshape(1, d), s_len=s_len, col0=n_main)
        x2d = _ffn(hf, x1, w_gate[l].astype(BF16), w_up[l].astype(BF16), w_down[l].astype(BF16),
                   norm_final.reshape(1, d), final_norm=(l == depth - 1))
    return x2d.reshape(bsz, s_len, d)
```

```python
import functools

import jax
import jax.numpy as jnp
from jax import lax
from jax.experimental import pallas as pl
from jax.experimental.pallas import tpu as pltpu

F32 = jnp.float32
BF16 = jnp.bfloat16

HEADS = 8
CHUNK = 128
CONV_K = 3
EPS = 1e-6
LANES = 128
VMEM_LIMIT_BYTES = 56 * 1024 * 1024

NT_DIMS = (((1,), (1,)), ((), ()))
TN_DIMS = (((0,), (0,)), ((), ()))


def _params(semantics):
    return pltpu.CompilerParams(dimension_semantics=semantics,
                                vmem_limit_bytes=VMEM_LIMIT_BYTES)


def _rms_scale(x):
    return lax.rsqrt(jnp.mean(x * x, axis=-1, keepdims=True) + EPS)


def _wprep_kernel(wt_ref, main_ref, gate_ref, *, n_main, gate_end):
    main_ref[:n_main, :] = wt_ref[:n_main, :].astype(BF16)
    main_ref[n_main:, :] = wt_ref[gate_end:, :].astype(BF16)
    gate_ref[...] = wt_ref[n_main:gate_end, :].astype(BF16)


def _wprep(w_t, *, n_main, gate_end):
    n_in, d = w_t.shape
    n_gate = gate_end - n_main
    assert n_main % 16 == 0 and n_gate % 16 == 0 and d % LANES == 0
    kern = functools.partial(_wprep_kernel, n_main=n_main, gate_end=gate_end)
    return pl.pallas_call(
        kern,
        grid=(d // LANES,),
        in_specs=[pl.BlockSpec((n_in, LANES), lambda i: (0, i))],
        out_specs=[pl.BlockSpec((n_in - n_gate, LANES), lambda i: (0, i)),
                   pl.BlockSpec((n_gate, LANES), lambda i: (0, i))],
        out_shape=[jax.ShapeDtypeStruct((n_in - n_gate, d), BF16),
                   jax.ShapeDtypeStruct((n_gate, d), BF16)],
        compiler_params=_params(("parallel",)),
        name="wprep",
    )(w_t)


def _inproj_kernel(x_ref, g_ref, wt_ref, wgate_ref, proj_ref, gcol_ref, h_ref, *, q_width, q_scale):
    j = pl.program_id(1)

    @pl.when(j == 0)
    def _():
        x = x_ref[...]
        h = (x * _rms_scale(x) * g_ref[...]).astype(BF16)
        h_ref[...] = h
        gcol_ref[...] = jnp.dot(h, wgate_ref[...], preferred_element_type=F32)

    acc = lax.dot_general(h_ref[...], wt_ref[...], NT_DIMS, preferred_element_type=F32)
    tn = acc.shape[1]
    col = j * tn + lax.broadcasted_iota(jnp.int32, (1, tn), 1)
    scale = jnp.where(col < q_width, q_scale, 1.0).astype(F32)
    proj_ref[...] = (acc * scale).astype(BF16)


def _inproj(x2d, g, w_main_t, w_gate, *, qk_width, q_scale, tm=1024, tn=2048):
    t, d = x2d.shape
    n = w_main_t.shape[0]
    assert t % tm == 0 and n % tn == 0
    kern = functools.partial(_inproj_kernel, q_width=qk_width, q_scale=q_scale)
    last_i = t // tm - 1
    x_map = lambda i, j: (jnp.where(j == 0, i, jnp.minimum(i + 1, last_i)), 0)
    return pl.pallas_call(
        kern,
        grid=(t // tm, n // tn),
        in_specs=[
            pl.BlockSpec((tm, d), x_map),
            pl.BlockSpec((1, d), lambda i, j: (0, 0)),
            pl.BlockSpec((tn, d), lambda i, j: (j, 0)),
            pl.BlockSpec((d, 2 * LANES), lambda i, j: (0, 0)),
        ],
        out_specs=[
            pl.BlockSpec((tm, tn), lambda i, j: (i, j)),
            pl.BlockSpec((tm, 2 * LANES), lambda i, j: (i, 0)),
        ],
        out_shape=[
            jax.ShapeDtypeStruct((t, n), BF16),
            jax.ShapeDtypeStruct((t, 2 * LANES), F32),
        ],
        scratch_shapes=[pltpu.VMEM((tm, d), BF16)],
        compiler_params=_params(("parallel", "arbitrary")),
        name="inproj",
    )(x2d, g, w_main_t, w_gate)


REP_D, REP_WK, REP_NM = range(3)


def _scan_rows(x, combine, identity):
    rows = lax.broadcasted_iota(jnp.int32, x.shape, 0)
    shift = 1
    while shift < x.shape[0]:
        x = combine(x, jnp.where(rows >= shift, pltpu.roll(x, shift, axis=0), identity))
        shift *= 2
    return x


def _mlstm_kernel(q_ref, k_ref, v_ref, o_ref, gcol_ref, bias_lane_ref, mh_ref, sel_ref,
                  out_ref, c_ref, n_ref, cn_ref, m_ref, s_ref, rep_ref, *, dk, dv):
    L = CHUNK

    @pl.when(pl.program_id(1) == 0)
    def _():
        c_ref[...] = jnp.zeros_like(c_ref)
        n_ref[...] = jnp.zeros_like(n_ref)
        cn_ref[...] = jnp.zeros_like(cn_ref)
        m_ref[...] = jnp.zeros_like(m_ref)

    heads = range(HEADS)
    blk = lambda h: slice(h * LANES, (h + 1) * LANES)
    qk_cols = lambda h: slice(h * dk, (h + 1) * dk)
    v_cols = lambda h: slice(h * dv, (h + 1) * dv)

    for h in heads:
        s_ref[h] = lax.dot_general(q_ref[:, qk_cols(h)], k_ref[:, qk_cols(h)], NT_DIMS,
                                   preferred_element_type=F32)

    row = lax.broadcasted_iota(jnp.int32, (L, L), 0)
    col = lax.broadcasted_iota(jnp.int32, (L, L), 1)
    causal = col <= row

    def replicate(slot, xcol):
        hi = xcol.astype(BF16)
        lo = (xcol - hi.astype(F32)).astype(BF16)
        rep_ref[slot] = jnp.dot(jnp.concatenate([hi, lo], axis=1), sel_ref[...],
                                preferred_element_type=F32)

    g2 = gcol_ref[...] + bias_lane_ref[...]
    b_c = _scan_rows(jax.nn.log_sigmoid(g2[:, LANES:]), jnp.add, 0.0)
    c_c = g2[:, :LANES] - b_c
    m_prev = m_ref[0:1, :]
    u_c = jnp.maximum(_scan_rows(c_c, jnp.maximum, -jnp.inf), m_prev)
    r = u_c[L - 1:L, :]
    m_ref[...] = jnp.broadcast_to(b_c[L - 1:L, :] + r, m_ref.shape)
    replicate(REP_D, r - u_c)
    replicate(REP_WK, c_c - r)
    replicate(REP_NM, -(b_c + u_c))
    crow = (c_c - r).T[:HEADS, :]
    mrow = jnp.broadcast_to(m_prev - r, (L, LANES)).T[:HEADS, :]
    decay = jnp.exp(mrow)

    ones = jnp.ones((L, LANES), BF16)
    for h in heads:
        d_h = rep_ref[REP_D, :, blk(h)]
        w = jnp.exp(jnp.where(causal, d_h + crow[h:h + 1, :], -jnp.inf))
        inter = jnp.exp(d_h + mrow[h:h + 1, :])
        lhs = jnp.concatenate([s_ref[h] * w, q_ref[:, qk_cols(h)].astype(F32) * inter], axis=1)
        rhs = jnp.concatenate([jnp.concatenate([v_ref[:, v_cols(h)], ones], axis=1), cn_ref[h]], axis=0)
        res = jnp.dot(lhs.astype(BF16), rhs, preferred_element_type=F32)
        num = res[:, :dv]
        inv = 1.0 / jnp.maximum(jnp.abs(res[:, dv:]), jnp.exp(rep_ref[REP_NM, :, blk(h)]))
        ms = jnp.mean(num * num, axis=-1, keepdims=True)
        rr = inv * lax.rsqrt(ms * inv * inv + EPS)
        gate = jax.nn.sigmoid(o_ref[:, v_cols(h)].astype(F32))
        hn = num * jnp.concatenate([rr, rr], axis=1) * mh_ref[:, v_cols(h)]
        out_ref[:, v_cols(h)] = (gate * hn).astype(out_ref.dtype)

    for h in heads:
        v = v_ref[:, v_cols(h)]
        wk = jnp.exp(rep_ref[REP_WK, :, blk(h)])
        wv = jnp.concatenate([wk, wk], axis=1) * v.astype(F32)
        upd = lax.dot_general(k_ref[:, qk_cols(h)], jnp.concatenate([wv, wk], axis=1).astype(BF16),
                              TN_DIMS, preferred_element_type=F32)
        dec = decay[h:h + 1, :]
        c_new = jnp.concatenate([dec, dec], axis=1) * c_ref[h] + upd[:, :dv]
        n_new = dec * n_ref[h] + upd[:, dv:]
        c_ref[h] = c_new
        n_ref[h] = n_new
        cn_ref[h] = jnp.concatenate([c_new, n_new], axis=1).astype(BF16)


def _mlstm(proj, gcol, bias_lane, mh, sel, *, bsz, s_len, qk_width, v_width):
    t = bsz * s_len
    nc = s_len // CHUNK
    dk, dv = qk_width // HEADS, v_width // HEADS
    assert v_width == 2 * qk_width and s_len % CHUNK == 0 and dk == CHUNK == LANES and dv == 2 * LANES
    rowblk = lambda b, c: b * nc + c
    const = lambda shape: pl.BlockSpec(shape, lambda b, c: (0, 0))
    kern = functools.partial(_mlstm_kernel, dk=dk, dv=dv)
    return pl.pallas_call(
        kern,
        grid=(bsz, nc),
        in_specs=[
            pl.BlockSpec((CHUNK, qk_width), lambda b, c: (rowblk(b, c), 0)),
            pl.BlockSpec((CHUNK, qk_width), lambda b, c: (rowblk(b, c), 1)),
            pl.BlockSpec((CHUNK, v_width), lambda b, c: (rowblk(b, c), 1)),
            pl.BlockSpec((CHUNK, v_width), lambda b, c: (rowblk(b, c), 2)),
            pl.BlockSpec((CHUNK, 2 * LANES), lambda b, c: (rowblk(b, c), 0)),
            const((1, 2 * LANES)),
            const((1, v_width)),
            const((2 * LANES, HEADS * LANES)),
        ],
        out_specs=pl.BlockSpec((CHUNK, v_width), lambda b, c: (rowblk(b, c), 0)),
        out_shape=jax.ShapeDtypeStruct((t, v_width), BF16),
        scratch_shapes=[
            pltpu.VMEM((HEADS, dk, dv), F32),
            pltpu.VMEM((HEADS, dk, LANES), F32),
            pltpu.VMEM((HEADS, dk, dv + LANES), BF16),
            pltpu.VMEM((8, LANES), F32),
            pltpu.VMEM((HEADS, CHUNK, CHUNK), F32),
            pltpu.VMEM((3, CHUNK, HEADS * LANES), F32),
        ],
        compiler_params=_params(("parallel", "arbitrary")),
        name="mlstm",
    )(proj, proj, proj, proj, gcol, bias_lane, mh, sel)


HALO = 16


def _mix_kernel(hm_ref, cb_ref, cc_ref, cx_ref, cc_halo_ref, cx_halo_ref, cw_ref, ga_ref, gb_ref,
                wa_ref, wb_ref, wo_ref, x_ref, g_ref, x1_ref, hf_ref, cbu_ref, *, blocks_per_seq, lane_chunk):
    i = pl.program_id(0)
    tm, width = cbu_ref.shape

    ya = jnp.dot(hm_ref[...], wa_ref[...], preferred_element_type=F32)

    keep_halo = (i % blocks_per_seq != 0).astype(F32)
    rows = lax.broadcasted_iota(jnp.int32, (tm, lane_chunk), 0)
    for c0 in range(0, width, lane_chunk):
        sl = slice(c0, c0 + lane_chunk)
        p = cc_ref[:, sl].astype(F32) * cx_ref[:, sl].astype(F32)
        ph = (cc_halo_ref[:, sl].astype(F32) * cx_halo_ref[:, sl].astype(F32)) * keep_halo
        p1 = jnp.where(rows == 0, ph[HALO - 1:HALO, :], pltpu.roll(p, 1, axis=0))
        p2 = jnp.where(rows == 0, ph[HALO - 2:HALO - 1, :],
                       jnp.where(rows == 1, ph[HALO - 1:HALO, :], pltpu.roll(p, 2, axis=0)))
        u = cw_ref[0:1, sl] * p2 + cw_ref[1:2, sl] * p1 + cw_ref[2:3, sl] * p
        cbu_ref[:, sl] = (cb_ref[:, sl].astype(F32) * u).astype(BF16)

    yb = jnp.dot(cbu_ref[...], wb_ref[...], preferred_element_type=F32)
    merged = (jax.nn.sigmoid(ga_ref[...].astype(F32)) * ya
              + jax.nn.sigmoid(gb_ref[...].astype(F32)) * yb)
    x1 = x_ref[...] + jnp.dot(merged.astype(BF16), wo_ref[...], preferred_element_type=F32)
    x1_ref[...] = x1
    hf_ref[...] = (x1 * _rms_scale(x1) * g_ref[...]).astype(hf_ref.dtype)


def _mix(hm, proj, conv_w, wa, wb, wo, x2d, g, *, s_len, col0, tm=256):
    t, width = hm.shape
    d = x2d.shape[1]
    assert t % tm == 0 and s_len % tm == 0 and col0 % width == 0 and width == d
    assert wa.shape == wb.shape == wo.shape == (width, d)
    cblk = col0 // width
    halo_blocks = tm // HALO
    kern = functools.partial(_mix_kernel, blocks_per_seq=s_len // tm, lane_chunk=256)
    halo_map = lambda off: (lambda i: (jnp.maximum(i * halo_blocks - 1, 0), cblk + off))
    col_block = lambda c: pl.BlockSpec((tm, width), lambda i: (i, c))
    resident = lambda shape: pl.BlockSpec(shape, lambda i: (0, 0), pipeline_mode=pl.Buffered(1))
    return pl.pallas_call(
        kern,
        grid=(t // tm,),
        in_specs=[
            col_block(0),
            col_block(cblk),
            col_block(cblk + 1),
            col_block(cblk + 2),
            pl.BlockSpec((HALO, width), halo_map(1)),
            pl.BlockSpec((HALO, width), halo_map(2)),
            pl.BlockSpec((CONV_K, width), lambda i: (0, 0)),
            col_block(cblk + 3),
            col_block(cblk + 4),
            resident((width, d)),
            resident((width, d)),
            resident((width, d)),
            col_block(0),
            pl.BlockSpec((1, d), lambda i: (0, 0)),
        ],
        out_specs=[col_block(0), col_block(0)],
        out_shape=[jax.ShapeDtypeStruct((t, d), F32), jax.ShapeDtypeStruct((t, d), BF16)],
        scratch_shapes=[pltpu.VMEM((tm, width), BF16)],
        compiler_params=_params(("parallel",)),
        name="mix",
    )(hm, proj, proj, proj, proj, proj, conv_w, proj, proj, wa, wb, wo, x2d, g)


def _ffn_kernel(hf_ref, x1_ref, wg_ref, wu_ref, wd_ref, gf_ref, out_ref, acc_ref, *, final_norm):
    f = pl.program_id(1)

    @pl.when(f == 0)
    def _():
        acc_ref[...] = jnp.zeros_like(acc_ref)

    hf = hf_ref[...]
    gate = jnp.dot(hf, wg_ref[...], preferred_element_type=F32)
    up = jnp.dot(hf, wu_ref[...], preferred_element_type=F32)
    act = (gate * jax.nn.sigmoid(gate) * up).astype(BF16)
    acc_ref[...] += jnp.dot(act, wd_ref[...], preferred_element_type=F32)

    @pl.when(f == pl.num_programs(1) - 1)
    def _():
        x2 = x1_ref[...] + acc_ref[...]
        if final_norm:
            x2 = x2 * _rms_scale(x2) * gf_ref[...]
        out_ref[...] = x2


def _ffn(hf, x1, wg, wu, wd, gf, *, final_norm, tm=512, tf=512):
    t, d = x1.shape
    dff = wg.shape[1]
    assert t % tm == 0 and dff % tf == 0 and dff // tf >= 3
    kern = functools.partial(_ffn_kernel, final_norm=final_norm)
    x1_map = lambda i, f: (jnp.where(f >= 2, i, jnp.maximum(i - 1, 0)), 0)
    return pl.pallas_call(
        kern,
        grid=(t // tm, dff // tf),
        in_specs=[
            pl.BlockSpec((tm, d), lambda i, f: (i, 0)),
            pl.BlockSpec((tm, d), x1_map),
            pl.BlockSpec((d, tf), lambda i, f: (0, f)),
            pl.BlockSpec((d, tf), lambda i, f: (0, f)),
            pl.BlockSpec((tf, d), lambda i, f: (f, 0)),
            pl.BlockSpec((1, d), lambda i, f: (0, 0)),
        ],
        out_specs=pl.BlockSpec((tm, d), lambda i, f: (i, 0)),
        out_shape=jax.ShapeDtypeStruct((t, d), F32),
        scratch_shapes=[pltpu.VMEM((tm, d), F32)],
        compiler_params=_params(("parallel", "arbitrary")),
        name="ffn",
    )(hf, x1, wg, wu, wd, gf)


def kernel(x, norm_mix, w_in, b_igate, b_fgate, conv_w, mh_norm, w_branch_a, w_branch_b,
           w_out, norm_ffn, w_gate, w_up, w_down, norm_final):
    bsz, s_len, d = x.shape
    depth = w_in.shape[0]
    v_width = mh_norm.shape[1]
    qk_width = v_width // 2
    conv_width = conv_w.shape[2]
    n_main = 2 * qk_width + 2 * v_width
    gate_end = n_main + 2 * HEADS
    assert w_in.shape[2] == gate_end + 3 * conv_width + 2 * d
    assert conv_w.shape[1] == CONV_K and b_igate.shape[1] == HEADS and conv_width == v_width
    q_scale = float((qk_width // HEADS) ** -0.5)

    lane = jnp.arange(2 * LANES)[:, None] % LANES
    sel = (lane == jnp.arange(HEADS * LANES)[None, :] // LANES).astype(BF16)
    pad_lanes = lambda a: jnp.pad(a, ((0, 0), (0, LANES - a.shape[1])))

    x2d = x.reshape(bsz * s_len, d)
    for l in range(depth):
        w_main_t, w_gate_t = _wprep(jnp.swapaxes(w_in[l], 0, 1), n_main=n_main, gate_end=gate_end)
        w_gate_col = jnp.concatenate([pad_lanes(w_gate_t[:HEADS].T), pad_lanes(w_gate_t[HEADS:].T)], axis=1)
        bias_lane = jnp.concatenate([pad_lanes(b_igate[l][None, :]), pad_lanes(b_fgate[l][None, :])], axis=1)

        proj, gcol = _inproj(x2d, norm_mix[l].reshape(1, d), w_main_t, w_gate_col,
                             qk_width=qk_width, q_scale=q_scale)
        hm = _mlstm(proj, gcol, bias_lane, mh_norm[l].reshape(1, v_width), sel,
                    bsz=bsz, s_len=s_len, qk_width=qk_width, v_width=v_width)
        x1, hf = _mix(hm, proj, conv_w[l], w_branch_a[l].astype(BF16), w_branch_b[l].astype(BF16),
                      w_out[l].astype(BF16), x2d, norm_ffn[l].reshape(1, d), s_len=s_len, col0=n_main)
        x2d = _ffn(hf, x1, w_gate[l].astype(BF16), w_up[l].astype(BF16), w_down[l].astype(BF16),
                   norm_final.reshape(1, d), final_norm=(l == depth - 1))
    return x2d.reshape(bsz, s_len, d)
```

```python
import functools

import jax
import jax.numpy as jnp
from jax import lax
from jax.experimental import pallas as pl
from jax.experimental.pallas import tpu as pltpu

F32 = jnp.float32
BF16 = jnp.bfloat16

HEADS = 8
CHUNK = 128
CONV_K = 3
EPS = 1e-6
LANES = 128
VMEM_LIMIT_BYTES = 60 * 1024 * 1024

NT_DIMS = (((1,), (1,)), ((), ()))
TN_DIMS = (((0,), (0,)), ((), ()))


def _params(semantics):
    return pltpu.CompilerParams(dimension_semantics=semantics,
                                vmem_limit_bytes=VMEM_LIMIT_BYTES)


def _rms_scale(x):
    return lax.rsqrt(jnp.mean(x * x, axis=-1, keepdims=True) + EPS)


def _wprep_kernel(wt_ref, main_ref, gate_ref, *, n_main, gate_end):
    main_ref[:n_main, :] = wt_ref[:n_main, :].astype(BF16)
    main_ref[n_main:, :] = wt_ref[gate_end:, :].astype(BF16)
    gate_ref[...] = wt_ref[n_main:gate_end, :].astype(BF16)


def _wprep(w_t, *, n_main, gate_end):
    n_in, d = w_t.shape
    n_gate = gate_end - n_main
    assert n_main % 16 == 0 and n_gate % 16 == 0 and d % LANES == 0
    kern = functools.partial(_wprep_kernel, n_main=n_main, gate_end=gate_end)
    return pl.pallas_call(
        kern,
        grid=(d // LANES,),
        in_specs=[pl.BlockSpec((n_in, LANES), lambda i: (0, i))],
        out_specs=[pl.BlockSpec((n_in - n_gate, LANES), lambda i: (0, i)),
                   pl.BlockSpec((n_gate, LANES), lambda i: (0, i))],
        out_shape=[jax.ShapeDtypeStruct((n_in - n_gate, d), BF16),
                   jax.ShapeDtypeStruct((n_gate, d), BF16)],
        compiler_params=_params(("parallel",)),
        name="wprep",
    )(w_t)


def _inproj_kernel(x_ref, g_ref, wt_ref, wgate_ref, proj_ref, gcol_ref, h_ref, *, q_width, q_scale):
    j = pl.program_id(1)

    @pl.when(j == 0)
    def _():
        x = x_ref[...]
        h = (x * _rms_scale(x) * g_ref[...]).astype(BF16)
        h_ref[...] = h
        gcol_ref[...] = jnp.dot(h, wgate_ref[...], preferred_element_type=F32)

    acc = lax.dot_general(h_ref[...], wt_ref[...], NT_DIMS, preferred_element_type=F32)
    tn = acc.shape[1]
    col = j * tn + lax.broadcasted_iota(jnp.int32, (1, tn), 1)
    scale = jnp.where(col < q_width, q_scale, 1.0).astype(F32)
    proj_ref[...] = (acc * scale).astype(BF16)


def _inproj(x2d, g, w_main_t, w_gate, *, qk_width, q_scale, tm=1024, tn=2048):
    t, d = x2d.shape
    n = w_main_t.shape[0]
    assert t % tm == 0 and n % tn == 0
    kern = functools.partial(_inproj_kernel, q_width=qk_width, q_scale=q_scale)
    last_i = t // tm - 1
    x_map = lambda i, j: (jnp.where(j == 0, i, jnp.minimum(i + 1, last_i)), 0)
    return pl.pallas_call(
        kern,
        grid=(t // tm, n // tn),
        in_specs=[
            pl.BlockSpec((tm, d), x_map),
            pl.BlockSpec((1, d), lambda i, j: (0, 0)),
            pl.BlockSpec((tn, d), lambda i, j: (j, 0)),
            pl.BlockSpec((d, 2 * LANES), lambda i, j: (0, 0)),
        ],
        out_specs=[
            pl.BlockSpec((tm, tn), lambda i, j: (i, j)),
            pl.BlockSpec((tm, 2 * LANES), lambda i, j: (i, 0)),
        ],
        out_shape=[
            jax.ShapeDtypeStruct((t, n), BF16),
            jax.ShapeDtypeStruct((t, 2 * LANES), F32),
        ],
        scratch_shapes=[pltpu.VMEM((tm, d), BF16)],
        compiler_params=_params(("parallel", "arbitrary")),
        name="inproj",
    )(x2d, g, w_main_t, w_gate)


REP_D, REP_WK, REP_NM = range(3)


def _scan_rows(x, combine, identity):
    rows = lax.broadcasted_iota(jnp.int32, x.shape, 0)
    shift = 1
    while shift < x.shape[0]:
        x = combine(x, jnp.where(rows >= shift, pltpu.roll(x, shift, axis=0), identity))
        shift *= 2
    return x


def _mlstm_kernel(q_ref, k_ref, v_ref, o_ref, gcol_ref, bias_lane_ref, mh_ref, sel_ref,
                  out_ref, c_ref, n_ref, cn_ref, m_ref, s_ref, rep_ref, *, dk, dv):
    L = CHUNK

    @pl.when(pl.program_id(1) == 0)
    def _():
        c_ref[...] = jnp.zeros_like(c_ref)
        n_ref[...] = jnp.zeros_like(n_ref)
        cn_ref[...] = jnp.zeros_like(cn_ref)
        m_ref[...] = jnp.zeros_like(m_ref)

    heads = range(HEADS)
    blk = lambda h: slice(h * LANES, (h + 1) * LANES)
    qk_cols = lambda h: slice(h * dk, (h + 1) * dk)
    v_cols = lambda h: slice(h * dv, (h + 1) * dv)

    for h in heads:
        s_ref[h] = lax.dot_general(q_ref[:, qk_cols(h)], k_ref[:, qk_cols(h)], NT_DIMS,
                                   preferred_element_type=F32)

    row = lax.broadcasted_iota(jnp.int32, (L, L), 0)
    col = lax.broadcasted_iota(jnp.int32, (L, L), 1)
    causal = col <= row

    def replicate(slot, xcol):
        hi = xcol.astype(BF16)
        lo = (xcol - hi.astype(F32)).astype(BF16)
        rep_ref[slot] = jnp.dot(jnp.concatenate([hi, lo], axis=1), sel_ref[...],
                                preferred_element_type=F32)

    g2 = gcol_ref[...] + bias_lane_ref[...]
    b_c = _scan_rows(jax.nn.log_sigmoid(g2[:, LANES:]), jnp.add, 0.0)
    c_c = g2[:, :LANES] - b_c
    m_prev = m_ref[0:1, :]
    u_c = jnp.maximum(_scan_rows(c_c, jnp.maximum, -jnp.inf), m_prev)
    r = u_c[L - 1:L, :]
    m_ref[...] = jnp.broadcast_to(b_c[L - 1:L, :] + r, m_ref.shape)
    replicate(REP_D, r - u_c)
    replicate(REP_WK, c_c - r)
    replicate(REP_NM, -(b_c + u_c))
    crow = (c_c - r).T[:HEADS, :]
    mrow = jnp.broadcast_to(m_prev - r, (L, LANES)).T[:HEADS, :]
    decay = jnp.exp(mrow)

    ones = jnp.ones((L, LANES), BF16)
    for h in heads:
        d_h = rep_ref[REP_D, :, blk(h)]
        w = jnp.exp(jnp.where(causal, d_h + crow[h:h + 1, :], -jnp.inf))
        inter = jnp.exp(d_h + mrow[h:h + 1, :])
        lhs = jnp.concatenate([s_ref[h] * w, q_ref[:, qk_cols(h)].astype(F32) * inter], axis=1)
        rhs = jnp.concatenate([jnp.concatenate([v_ref[:, v_cols(h)], ones], axis=1), cn_ref[h]], axis=0)
        res = jnp.dot(lhs.astype(BF16), rhs, preferred_element_type=F32)
        num = res[:, :dv]
        inv = 1.0 / jnp.maximum(jnp.abs(res[:, dv:]), jnp.exp(rep_ref[REP_NM, :, blk(h)]))
        ms = jnp.mean(num * num, axis=-1, keepdims=True)
        rr = inv * lax.rsqrt(ms * inv * inv + EPS)
        gate = jax.nn.sigmoid(o_ref[:, v_cols(h)].astype(F32))
        hn = num * jnp.concatenate([rr, rr], axis=1) * mh_ref[:, v_cols(h)]
        out_ref[:, v_cols(h)] = (gate * hn).astype(out_ref.dtype)

    for h in heads:
        v = v_ref[:, v_cols(h)]
        wk = jnp.exp(rep_ref[REP_WK, :, blk(h)])
        wv = jnp.concatenate([wk, wk], axis=1) * v.astype(F32)
        upd = lax.dot_general(k_ref[:, qk_cols(h)], jnp.concatenate([wv, wk], axis=1).astype(BF16),
                              TN_DIMS, preferred_element_type=F32)
        dec = decay[h:h + 1, :]
        c_new = jnp.concatenate([dec, dec], axis=1) * c_ref[h] + upd[:, :dv]
        n_new = dec * n_ref[h] + upd[:, dv:]
        c_ref[h] = c_new
        n_ref[h] = n_new
        cn_ref[h] = jnp.concatenate([c_new, n_new], axis=1).astype(BF16)


def _mlstm(proj, gcol, bias_lane, mh, sel, *, bsz, s_len, qk_width, v_width):
    t = bsz * s_len
    nc = s_len // CHUNK
    dk, dv = qk_width // HEADS, v_width // HEADS
    assert v_width == 2 * qk_width and s_len % CHUNK == 0 and dk == CHUNK == LANES and dv == 2 * LANES
    rowblk = lambda b, c: b * nc + c
    const = lambda shape: pl.BlockSpec(shape, lambda b, c: (0, 0))
    kern = functools.partial(_mlstm_kernel, dk=dk, dv=dv)
    return pl.pallas_call(
        kern,
        grid=(bsz, nc),
        in_specs=[
            pl.BlockSpec((CHUNK, qk_width), lambda b, c: (rowblk(b, c), 0)),
            pl.BlockSpec((CHUNK, qk_width), lambda b, c: (rowblk(b, c), 1)),
            pl.BlockSpec((CHUNK, v_width), lambda b, c: (rowblk(b, c), 1)),
            pl.BlockSpec((CHUNK, v_width), lambda b, c: (rowblk(b, c), 2)),
            pl.BlockSpec((CHUNK, 2 * LANES), lambda b, c: (rowblk(b, c), 0)),
            const((1, 2 * LANES)),
            const((1, v_width)),
            const((2 * LANES, HEADS * LANES)),
        ],
        out_specs=pl.BlockSpec((CHUNK, v_width), lambda b, c: (rowblk(b, c), 0)),
        out_shape=jax.ShapeDtypeStruct((t, v_width), BF16),
        scratch_shapes=[
            pltpu.VMEM((HEADS, dk, dv), F32),
            pltpu.VMEM((HEADS, dk, LANES), F32),
            pltpu.VMEM((HEADS, dk, dv + LANES), BF16),
            pltpu.VMEM((8, LANES), F32),
            pltpu.VMEM((HEADS, CHUNK, CHUNK), F32),
            pltpu.VMEM((3, CHUNK, HEADS * LANES), F32),
        ],
        compiler_params=_params(("parallel", "arbitrary")),
        name="mlstm",
    )(proj, proj, proj, proj, gcol, bias_lane, mh, sel)


HALO = 16


def _mix_kernel(hm_ref, cb_ref, cc_ref, cx_ref, cc_halo_ref, cx_halo_ref, cw_ref, ga_ref, gb_ref,
                wa_ref, wb_ref, wo_ref, x_ref, g_ref, x1_ref, hf_ref, cbu_ref, *, blocks_per_seq, lane_chunk):
    i = pl.program_id(0)
    tm, width = cbu_ref.shape

    ya = jnp.dot(hm_ref[...], wa_ref[...], preferred_element_type=F32)

    keep_halo = (i % blocks_per_seq != 0).astype(F32)
    rows = lax.broadcasted_iota(jnp.int32, (tm, lane_chunk), 0)
    for c0 in range(0, width, lane_chunk):
        sl = slice(c0, c0 + lane_chunk)
        p = cc_ref[:, sl].astype(F32) * cx_ref[:, sl].astype(F32)
        ph = (cc_halo_ref[:, sl].astype(F32) * cx_halo_ref[:, sl].astype(F32)) * keep_halo
        p1 = jnp.where(rows == 0, ph[HALO - 1:HALO, :], pltpu.roll(p, 1, axis=0))
        p2 = jnp.where(rows == 0, ph[HALO - 2:HALO - 1, :],
                       jnp.where(rows == 1, ph[HALO - 1:HALO, :], pltpu.roll(p, 2, axis=0)))
        u = cw_ref[0:1, sl] * p2 + cw_ref[1:2, sl] * p1 + cw_ref[2:3, sl] * p
        cbu_ref[:, sl] = (cb_ref[:, sl].astype(F32) * u).astype(BF16)

    yb = jnp.dot(cbu_ref[...], wb_ref[...], preferred_element_type=F32)
    merged = (jax.nn.sigmoid(ga_ref[...].astype(F32)) * ya
              + jax.nn.sigmoid(gb_ref[...].astype(F32)) * yb)
    x1 = x_ref[...] + jnp.dot(merged.astype(BF16), wo_ref[...], preferred_element_type=F32)
    x1_ref[...] = x1
    hf_ref[...] = (x1 * _rms_scale(x1) * g_ref[...]).astype(hf_ref.dtype)


def _mix(hm, proj, conv_w, wa, wb, wo, x2d, g, *, s_len, col0, tm=256):
    t, width = hm.shape
    d = x2d.shape[1]
    assert t % tm == 0 and s_len % tm == 0 and col0 % width == 0 and width == d
    assert wa.shape == wb.shape == wo.shape == (width, d)
    cblk = col0 // width
    halo_blocks = tm // HALO
    kern = functools.partial(_mix_kernel, blocks_per_seq=s_len // tm, lane_chunk=256)
    halo_map = lambda off: (lambda i: (jnp.maximum(i * halo_blocks - 1, 0), cblk + off))
    col_block = lambda c: pl.BlockSpec((tm, width), lambda i: (i, c))
    resident = lambda shape: pl.BlockSpec(shape, lambda i: (0, 0), pipeline_mode=pl.Buffered(1))
    return pl.pallas_call(
        kern,
        grid=(t // tm,),
        in_specs=[
            col_block(0),
            col_block(cblk),
            col_block(cblk + 1),
            col_block(cblk + 2),
            pl.BlockSpec((HALO, width), halo_map(1)),
            pl.BlockSpec((HALO, width), halo_map(2)),
            pl.BlockSpec((CONV_K, width), lambda i: (0, 0)),
            col_block(cblk + 3),
            col_block(cblk + 4),
            resident((width, d)),
            resident((width, d)),
            resident((width, d)),
            col_block(0),
            pl.BlockSpec((1, d), lambda i: (0, 0)),
        ],
        out_specs=[col_block(0), col_block(0)],
        out_shape=[jax.ShapeDtypeStruct((t, d), F32), jax.ShapeDtypeStruct((t, d), BF16)],
        scratch_shapes=[pltpu.VMEM((tm, width), BF16)],
        compiler_params=_params(("parallel",)),
        name="mix",
    )(hm, proj, proj, proj, proj, proj, conv_w, proj, proj, wa, wb, wo, x2d, g)


def _ffn_kernel(hf_ref, x1_hbm, wg_ref, wu_ref, wd_ref, gf_ref, out_ref, x1_buf, x1_sem,
                *, final_norm, epilogue_rows):
    i = pl.program_id(0)
    f = pl.program_id(1)
    tm = out_ref.shape[0]
    x1_copy = pltpu.make_async_copy(x1_hbm.at[pl.ds(pl.multiple_of(i * tm, tm), tm), :], x1_buf, x1_sem)

    @pl.when(f == 0)
    def _():
        x1_copy.start()
        out_ref[...] = jnp.zeros_like(out_ref)

    hf = hf_ref[...]
    gate = jnp.dot(hf, wg_ref[...], preferred_element_type=F32)
    up = jnp.dot(hf, wu_ref[...], preferred_element_type=F32)
    act = (gate * jax.nn.sigmoid(gate) * up).astype(BF16)
    out_ref[...] += jnp.dot(act, wd_ref[...], preferred_element_type=F32)

    @pl.when(f == pl.num_programs(1) - 1)
    def _():
        x1_copy.wait()
        for r0 in range(0, tm, epilogue_rows):
            rows = slice(r0, r0 + epilogue_rows)
            x2 = x1_buf[rows, :] + out_ref[rows, :]
            if final_norm:
                x2 = x2 * _rms_scale(x2) * gf_ref[...]
            out_ref[rows, :] = x2


def _ffn(hf, x1, wg, wu, wd, gf, *, final_norm, tm=1024, tf=512, epilogue_rows=128):
    t, d = x1.shape
    dff = wg.shape[1]
    assert t % tm == 0 and dff % tf == 0 and tm % epilogue_rows == 0
    kern = functools.partial(_ffn_kernel, final_norm=final_norm, epilogue_rows=epilogue_rows)
    return pl.pallas_call(
        kern,
        grid=(t // tm, dff // tf),
        in_specs=[
            pl.BlockSpec((tm, d), lambda i, f: (i, 0)),
            pl.BlockSpec(memory_space=pl.ANY),
            pl.BlockSpec((d, tf), lambda i, f: (0, f)),
            pl.BlockSpec((d, tf), lambda i, f: (0, f)),
            pl.BlockSpec((tf, d), lambda i, f: (f, 0)),
            pl.BlockSpec((1, d), lambda i, f: (0, 0)),
        ],
        out_specs=pl.BlockSpec((tm, d), lambda i, f: (i, 0)),
        out_shape=jax.ShapeDtypeStruct((t, d), F32),
        scratch_shapes=[pltpu.VMEM((tm, d), F32), pltpu.SemaphoreType.DMA(())],
        compiler_params=_params(("arbitrary", "arbitrary")),
        name="ffn",
    )(hf, x1, wg, wu, wd, gf)


def kernel(x, norm_mix, w_in, b_igate, b_fgate, conv_w, mh_norm, w_branch_a, w_branch_b,
           w_out, norm_ffn, w_gate, w_up, w_down, norm_final):
    bsz, s_len, d = x.shape
    depth = w_in.shape[0]
    v_width = mh_norm.shape[1]
    qk_width = v_width // 2
    conv_width = conv_w.shape[2]
    n_main = 2 * qk_width + 2 * v_width
    gate_end = n_main + 2 * HEADS
    assert w_in.shape[2] == gate_end + 3 * conv_width + 2 * d
    assert conv_w.shape[1] == CONV_K and b_igate.shape[1] == HEADS and conv_width == v_width
    q_scale = float((qk_width // HEADS) ** -0.5)

    lane = jnp.arange(2 * LANES)[:, None] % LANES
    sel = (lane == jnp.arange(HEADS * LANES)[None, :] // LANES).astype(BF16)
    pad_lanes = lambda a: jnp.pad(a, ((0, 0), (0, LANES - a.shape[1])))

    x2d = x.reshape(bsz * s_len, d)
    for l in range(depth):
        w_main_t, w_gate_t = _wprep(jnp.swapaxes(w_in[l], 0, 1), n_main=n_main, gate_end=gate_end)
        w_gate_col = jnp.concatenate([pad_lanes(w_gate_t[:HEADS].T), pad_lanes(w_gate_t[HEADS:].T)], axis=1)
        bias_lane = jnp.concatenate([pad_lanes(b_igate[l][None, :]), pad_lanes(b_fgate[l][None, :])], axis=1)

        proj, gcol = _inproj(x2d, norm_mix[l].reshape(1, d), w_main_t, w_gate_col,
                             qk_width=qk_width, q_scale=q_scale)
        hm = _mlstm(proj, gcol, bias_lane, mh_norm[l].reshape(1, v_width), sel,
                    bsz=bsz, s_len=s_len, qk_width=qk_width, v_width=v_width)
        x1, hf = _mix(hm, proj, conv_w[l], w_branch_a[l].astype(BF16), w_branch_b[l].astype(BF16),
                      w_out[l].astype(BF16), x2d, norm_ffn[l].reshape(1, d), s_len=s_len, col0=n_main)
        x2d = _ffn(hf, x1, w_gate[l].astype(BF16), w_up[l].astype(BF16), w_down[l].astype(BF16),
                   norm_final.reshape(1, d), final_norm=(l == depth - 1))
    return x2d.reshape(bsz, s_len, d)
```

```python
import functools

import jax
import jax.numpy as jnp
from jax import lax
from jax.experimental import pallas as pl
from jax.experimental.pallas import tpu as pltpu

F32 = jnp.float32
BF16 = jnp.bfloat16

HEADS = 8
CHUNK = 128
CONV_K = 3
EPS = 1e-6
LANES = 128
VMEM_LIMIT_BYTES = 60 * 1024 * 1024

NT_DIMS = (((1,), (1,)), ((), ()))
TN_DIMS = (((0,), (0,)), ((), ()))


def _params(semantics):
    return pltpu.CompilerParams(dimension_semantics=semantics,
                                vmem_limit_bytes=VMEM_LIMIT_BYTES)


def _rms_scale(x):
    return lax.rsqrt(jnp.mean(x * x, axis=-1, keepdims=True) + EPS)


def _wprep_kernel(wt_ref, main_ref, gate_ref, *, n_main, gate_end):
    main_ref[:n_main, :] = wt_ref[:n_main, :].astype(BF16)
    main_ref[n_main:, :] = wt_ref[gate_end:, :].astype(BF16)
    gate_ref[...] = wt_ref[n_main:gate_end, :].astype(BF16)


def _wprep(w_t, *, n_main, gate_end):
    n_in, d = w_t.shape
    n_gate = gate_end - n_main
    assert n_main % 16 == 0 and n_gate % 16 == 0 and d % LANES == 0
    kern = functools.partial(_wprep_kernel, n_main=n_main, gate_end=gate_end)
    return pl.pallas_call(
        kern,
        grid=(d // LANES,),
        in_specs=[pl.BlockSpec((n_in, LANES), lambda i: (0, i))],
        out_specs=[pl.BlockSpec((n_in - n_gate, LANES), lambda i: (0, i)),
                   pl.BlockSpec((n_gate, LANES), lambda i: (0, i))],
        out_shape=[jax.ShapeDtypeStruct((n_in - n_gate, d), BF16),
                   jax.ShapeDtypeStruct((n_gate, d), BF16)],
        compiler_params=_params(("parallel",)),
        name="wprep",
    )(w_t)


def _inproj_kernel(x_ref, g_ref, wt_ref, wgate_ref, *refs, q_width, q_scale, n_side):
    side_in, (proj_ref, gcol_ref), side_out, (h_ref,) = (
        refs[:n_side], refs[n_side:n_side + 2], refs[n_side + 2:2 * n_side + 2], refs[2 * n_side + 2:])
    j = pl.program_id(1)

    @pl.when(j == 0)
    def _():
        x = x_ref[...]
        h = (x * _rms_scale(x) * g_ref[...]).astype(BF16)
        h_ref[...] = h
        gcol_ref[...] = jnp.dot(h, wgate_ref[...], preferred_element_type=F32)

    acc = lax.dot_general(h_ref[...], wt_ref[...], NT_DIMS, preferred_element_type=F32)
    tn = acc.shape[1]
    col = j * tn + lax.broadcasted_iota(jnp.int32, (1, tn), 1)
    scale = jnp.where(col < q_width, q_scale, 1.0).astype(F32)
    proj_ref[...] = (acc * scale).astype(BF16)

    for w_ref, wb_ref in zip(side_in, side_out):
        wb_ref[...] = w_ref[...].astype(BF16)


def _side_chunk_rows(rows, n_steps):
    chunk = -(-rows // n_steps)
    chunk += -chunk % 16
    while rows % chunk:
        chunk += 16
    return chunk


def _inproj(x2d, g, w_main_t, w_gate, side_weights, *, qk_width, q_scale, tm=1024, tn=2048):
    t, d = x2d.shape
    n = w_main_t.shape[0]
    assert t % tm == 0 and n % tn == 0
    n_i, n_j = t // tm, n // tn
    kern = functools.partial(_inproj_kernel, q_width=qk_width, q_scale=q_scale, n_side=len(side_weights))
    x_map = lambda i, j: (jnp.where(j == 0, i, jnp.minimum(i + 1, n_i - 1)), 0)

    def side_spec(w):
        chunk = _side_chunk_rows(w.shape[0], n_i * n_j)
        last = w.shape[0] // chunk - 1
        return pl.BlockSpec((chunk, w.shape[1]), lambda i, j: (jnp.minimum(i * n_j + j, last), 0))

    side_specs = [side_spec(w) for w in side_weights]
    outs = pl.pallas_call(
        kern,
        grid=(n_i, n_j),
        in_specs=[
            pl.BlockSpec((tm, d), x_map),
            pl.BlockSpec((1, d), lambda i, j: (0, 0)),
            pl.BlockSpec((tn, d), lambda i, j: (j, 0)),
            pl.BlockSpec((d, 2 * LANES), lambda i, j: (0, 0)),
            *side_specs,
        ],
        out_specs=[
            pl.BlockSpec((tm, tn), lambda i, j: (i, j)),
            pl.BlockSpec((tm, 2 * LANES), lambda i, j: (i, 0)),
            *side_specs,
        ],
        out_shape=[
            jax.ShapeDtypeStruct((t, n), BF16),
            jax.ShapeDtypeStruct((t, 2 * LANES), F32),
            *[jax.ShapeDtypeStruct(w.shape, BF16) for w in side_weights],
        ],
        scratch_shapes=[pltpu.VMEM((tm, d), BF16)],
        compiler_params=_params(("arbitrary", "arbitrary")),
        name="inproj",
    )(x2d, g, w_main_t, w_gate, *side_weights)
    return outs[0], outs[1], outs[2:]


REP_D, REP_WK, REP_NM = range(3)
SEQS = 1
LOG2E = 1.4426950408889634


def _scan_rows(x, combine, identity):
    rows = lax.broadcasted_iota(jnp.int32, x.shape, 0)
    shift = 1
    while shift < x.shape[0]:
        x = combine(x, jnp.where(rows >= shift, pltpu.roll(x, shift, axis=0), identity))
        shift *= 2
    return x


def _mlstm_kernel(q_ref, k_ref, v_ref, o_ref, gcol_ref, bias_lane_ref, mh_ref, sel_ref,
                  out_ref, c_ref, n_ref, cn_ref, m_ref, s_ref, rep_ref, *, dk, dv):
    L = CHUNK

    @pl.when(pl.program_id(1) == 0)
    def _():
        c_ref[...] = jnp.zeros_like(c_ref)
        n_ref[...] = jnp.zeros_like(n_ref)
        cn_ref[...] = jnp.zeros_like(cn_ref)
        m_ref[...] = jnp.zeros_like(m_ref)

    seqs = range(SEQS)
    heads = range(HEADS)
    blk = lambda h: slice(h * LANES, (h + 1) * LANES)
    qk_cols = lambda h: slice(h * dk, (h + 1) * dk)
    v_cols = lambda h: slice(h * dv, (h + 1) * dv)

    for sq in seqs:
        for h in heads:
            s_ref[sq, h] = lax.dot_general(q_ref[sq, :, qk_cols(h)], k_ref[sq, :, qk_cols(h)], NT_DIMS,
                                           preferred_element_type=F32)

    row = lax.broadcasted_iota(jnp.int32, (L, L), 0)
    col = lax.broadcasted_iota(jnp.int32, (L, L), 1)
    causal = col <= row

    def replicate(sq, slot, xcol):
        hi = xcol.astype(BF16)
        lo = (xcol - hi.astype(F32)).astype(BF16)
        rep_ref[sq, slot] = jnp.dot(jnp.concatenate([hi, lo], axis=1), sel_ref[...],
                                    preferred_element_type=F32)

    crow, mrow, decay = [], [], []
    for sq in seqs:
        g2 = gcol_ref[sq] + bias_lane_ref[...]
        b_c = _scan_rows(jax.nn.log_sigmoid(g2[:, LANES:]), jnp.add, 0.0)
        c_c = g2[:, :LANES] - b_c
        m_prev = m_ref[sq, 0:1, :]
        u_c = jnp.maximum(_scan_rows(c_c, jnp.maximum, -jnp.inf), m_prev)
        r = u_c[L - 1:L, :]
        m_ref[sq] = jnp.broadcast_to(b_c[L - 1:L, :] + r, m_ref.shape[1:])
        replicate(sq, REP_D, (r - u_c) * LOG2E)
        replicate(sq, REP_WK, (c_c - r) * LOG2E)
        replicate(sq, REP_NM, -(b_c + u_c) * LOG2E)
        crow.append(((c_c - r) * LOG2E).T[:HEADS, :])
        mrow.append(jnp.broadcast_to((m_prev - r) * LOG2E, (L, LANES)).T[:HEADS, :])
        decay.append(jnp.exp2(mrow[sq]))

    ones = jnp.ones((L, LANES), BF16)
    for sq in seqs:
        for h in heads:
            d_h = rep_ref[sq, REP_D, :, blk(h)]
            w = jnp.exp2(jnp.where(causal, d_h + crow[sq][h:h + 1, :], -jnp.inf))
            inter = jnp.exp2(d_h + mrow[sq][h:h + 1, :])
            lhs = jnp.concatenate([s_ref[sq, h] * w, q_ref[sq, :, qk_cols(h)].astype(F32) * inter], axis=1)
            rhs = jnp.concatenate([jnp.concatenate([v_ref[sq, :, v_cols(h)], ones], axis=1), cn_ref[sq, h]],
                                  axis=0)
            res = jnp.dot(lhs.astype(BF16), rhs, preferred_element_type=F32)
            num = res[:, :dv]
            inv = 1.0 / jnp.maximum(jnp.abs(res[:, dv:]), jnp.exp2(rep_ref[sq, REP_NM, :, blk(h)]))
            ms = jnp.mean(num * num, axis=-1, keepdims=True)
            rr = inv * lax.rsqrt(ms * inv * inv + EPS)
            gate = jax.nn.sigmoid(o_ref[sq, :, v_cols(h)].astype(F32))
            hn = num * jnp.concatenate([rr, rr], axis=1) * mh_ref[:, v_cols(h)]
            out_ref[sq, :, v_cols(h)] = (gate * hn).astype(out_ref.dtype)

    for sq in seqs:
        for h in heads:
            v = v_ref[sq, :, v_cols(h)]
            wk = jnp.exp2(rep_ref[sq, REP_WK, :, blk(h)])
            wv = jnp.concatenate([wk, wk], axis=1) * v.astype(F32)
            upd = lax.dot_general(k_ref[sq, :, qk_cols(h)], jnp.concatenate([wv, wk], axis=1).astype(BF16),
                                  TN_DIMS, preferred_element_type=F32)
            dec = decay[sq][h:h + 1, :]
            c_new = jnp.concatenate([dec, dec], axis=1) * c_ref[sq, h] + upd[:, :dv]
            n_new = dec * n_ref[sq, h] + upd[:, dv:]
            c_ref[sq, h] = c_new
            n_ref[sq, h] = n_new
            cn_ref[sq, h] = jnp.concatenate([c_new, n_new], axis=1).astype(BF16)


def _mlstm(proj, gcol, bias_lane, mh, sel, *, bsz, s_len, qk_width, v_width):
    nc = s_len // CHUNK
    dk, dv = qk_width // HEADS, v_width // HEADS
    assert v_width == 2 * qk_width and s_len % CHUNK == 0 and dk == CHUNK == LANES and dv == 2 * LANES
    assert bsz % SEQS == 0
    proj3 = proj.reshape(bsz, s_len, proj.shape[1])
    gcol3 = gcol.reshape(bsz, s_len, gcol.shape[1])
    step = lambda width, colblk: pl.BlockSpec((SEQS, CHUNK, width), lambda b, c: (b, c, colblk))
    const = lambda shape: pl.BlockSpec(shape, lambda b, c: (0, 0))
    kern = functools.partial(_mlstm_kernel, dk=dk, dv=dv)
    hm = pl.pallas_call(
        kern,
        grid=(bsz // SEQS, nc),
        in_specs=[
            step(qk_width, 0),
            step(qk_width, 1),
            step(v_width, 1),
            step(v_width, 2),
            step(2 * LANES, 0),
            const((1, 2 * LANES)),
            const((1, v_width)),
            const((2 * LANES, HEADS * LANES)),
        ],
        out_specs=step(v_width, 0),
        out_shape=jax.ShapeDtypeStruct((bsz, s_len, v_width), BF16),
        scratch_shapes=[
            pltpu.VMEM((SEQS, HEADS, dk, dv), F32),
            pltpu.VMEM((SEQS, HEADS, dk, LANES), F32),
            pltpu.VMEM((SEQS, HEADS, dk, dv + LANES), BF16),
            pltpu.VMEM((SEQS, 8, LANES), F32),
            pltpu.VMEM((SEQS, HEADS, CHUNK, CHUNK), F32),
            pltpu.VMEM((SEQS, 3, CHUNK, HEADS * LANES), F32),
        ],
        compiler_params=_params(("parallel", "arbitrary")),
        name="mlstm",
    )(proj3, proj3, proj3, proj3, gcol3, bias_lane, mh, sel)
    return hm.reshape(bsz * s_len, v_width)


HALO = 16


def _mix_kernel(hm_ref, cb_ref, cc_ref, cx_ref, cc_halo_ref, cx_halo_ref, cw_ref, ga_ref, gb_ref,
                wa_ref, wb_ref, wo_ref, x_ref, g_ref, x1_ref, hf_ref, cbu_ref, *, blocks_per_seq, lane_chunk):
    i = pl.program_id(0)
    tm, width = cbu_ref.shape

    ya = jnp.dot(hm_ref[...], wa_ref[...], preferred_element_type=F32)

    keep_halo = (i % blocks_per_seq != 0).astype(F32)
    rows = lax.broadcasted_iota(jnp.int32, (tm, lane_chunk), 0)
    for c0 in range(0, width, lane_chunk):
        sl = slice(c0, c0 + lane_chunk)
        p = cc_ref[:, sl].astype(F32) * cx_ref[:, sl].astype(F32)
        ph = (cc_halo_ref[:, sl].astype(F32) * cx_halo_ref[:, sl].astype(F32)) * keep_halo
        p1 = jnp.where(rows == 0, ph[HALO - 1:HALO, :], pltpu.roll(p, 1, axis=0))
        p2 = jnp.where(rows == 0, ph[HALO - 2:HALO - 1, :],
                       jnp.where(rows == 1, ph[HALO - 1:HALO, :], pltpu.roll(p, 2, axis=0)))
        u = cw_ref[0:1, sl] * p2 + cw_ref[1:2, sl] * p1 + cw_ref[2:3, sl] * p
        cbu_ref[:, sl] = (cb_ref[:, sl].astype(F32) * u).astype(BF16)

    yb = jnp.dot(cbu_ref[...], wb_ref[...], preferred_element_type=F32)
    merged = (jax.nn.sigmoid(ga_ref[...].astype(F32)) * ya
              + jax.nn.sigmoid(gb_ref[...].astype(F32)) * yb)
    x1 = x_ref[...] + jnp.dot(merged.astype(BF16), wo_ref[...], preferred_element_type=F32)
    x1_ref[...] = x1
    hf_ref[...] = (x1 * _rms_scale(x1) * g_ref[...]).astype(hf_ref.dtype)


def _mix(hm, proj, conv_w, wa, wb, wo, x2d, g, *, s_len, col0, tm=256):
    t, width = hm.shape
    d = x2d.shape[1]
    assert t % tm == 0 and s_len % tm == 0 and col0 % width == 0 and width == d
    assert wa.shape == wb.shape == wo.shape == (width, d)
    cblk = col0 // width
    halo_blocks = tm // HALO
    kern = functools.partial(_mix_kernel, blocks_per_seq=s_len // tm, lane_chunk=256)
    halo_map = lambda off: (lambda i: (jnp.maximum(i * halo_blocks - 1, 0), cblk + off))
    col_block = lambda c: pl.BlockSpec((tm, width), lambda i: (i, c))
    resident = lambda shape: pl.BlockSpec(shape, lambda i: (0, 0), pipeline_mode=pl.Buffered(1))
    return pl.pallas_call(
        kern,
        grid=(t // tm,),
        in_specs=[
            col_block(0),
            col_block(cblk),
            col_block(cblk + 1),
            col_block(cblk + 2),
            pl.BlockSpec((HALO, width), halo_map(1)),
            pl.BlockSpec((HALO, width), halo_map(2)),
            pl.BlockSpec((CONV_K, width), lambda i: (0, 0)),
            col_block(cblk + 3),
            col_block(cblk + 4),
            resident((width, d)),
            resident((width, d)),
            resident((width, d)),
            col_block(0),
            pl.BlockSpec((1, d), lambda i: (0, 0)),
        ],
        out_specs=[col_block(0), col_block(0)],
        out_shape=[jax.ShapeDtypeStruct((t, d), F32), jax.ShapeDtypeStruct((t, d), BF16)],
        scratch_shapes=[pltpu.VMEM((tm, width), BF16)],
        compiler_params=_params(("parallel",)),
        name="mix",
    )(hm, proj, proj, proj, proj, proj, conv_w, proj, proj, wa, wb, wo, x2d, g)


def _ffn_kernel(hf_ref, x1_hbm, wg_ref, wu_ref, wd_ref, gf_ref, out_ref, x1_buf, x1_sem,
                *, final_norm, epilogue_rows):
    i = pl.program_id(0)
    f = pl.program_id(1)
    tm = out_ref.shape[0]
    x1_copy = pltpu.make_async_copy(x1_hbm.at[pl.ds(pl.multiple_of(i * tm, tm), tm), :], x1_buf, x1_sem)

    @pl.when(f == 0)
    def _():
        x1_copy.start()
        out_ref[...] = jnp.zeros_like(out_ref)

    hf = hf_ref[...]
    gate = jnp.dot(hf, wg_ref[...], preferred_element_type=F32)
    up = jnp.dot(hf, wu_ref[...], preferred_element_type=F32)
    act = (gate * jax.nn.sigmoid(gate) * up).astype(BF16)
    out_ref[...] += jnp.dot(act, wd_ref[...], preferred_element_type=F32)

    @pl.when(f == pl.num_programs(1) - 1)
    def _():
        x1_copy.wait()
        for r0 in range(0, tm, epilogue_rows):
            rows = slice(r0, r0 + epilogue_rows)
            x2 = x1_buf[rows, :] + out_ref[rows, :]
            if final_norm:
                x2 = x2 * _rms_scale(x2) * gf_ref[...]
            out_ref[rows, :] = x2


def _ffn(hf, x1, wg, wu, wd, gf, *, final_norm, tm=1024, tf=512, epilogue_rows=128):
    t, d = x1.shape
    dff = wg.shape[1]
    assert t % tm == 0 and dff % tf == 0 and tm % epilogue_rows == 0
    kern = functools.partial(_ffn_kernel, final_norm=final_norm, epilogue_rows=epilogue_rows)
    return pl.pallas_call(
        kern,
        grid=(t // tm, dff // tf),
        in_specs=[
            pl.BlockSpec((tm, d), lambda i, f: (i, 0)),
            pl.BlockSpec(memory_space=pl.ANY),
            pl.BlockSpec((d, tf), lambda i, f: (0, f)),
            pl.BlockSpec((d, tf), lambda i, f: (0, f)),
            pl.BlockSpec((tf, d), lambda i, f: (f, 0)),
            pl.BlockSpec((1, d), lambda i, f: (0, 0)),
        ],
        out_specs=pl.BlockSpec((tm, d), lambda i, f: (i, 0)),
        out_shape=jax.ShapeDtypeStruct((t, d), F32),
        scratch_shapes=[pltpu.VMEM((tm, d), F32), pltpu.SemaphoreType.DMA(())],
        compiler_params=_params(("arbitrary", "arbitrary")),
        name="ffn",
    )(hf, x1, wg, wu, wd, gf)


def kernel(x, norm_mix, w_in, b_igate, b_fgate, conv_w, mh_norm, w_branch_a, w_branch_b,
           w_out, norm_ffn, w_gate, w_up, w_down, norm_final):
    bsz, s_len, d = x.shape
    depth = w_in.shape[0]
    v_width = mh_norm.shape[1]
    qk_width = v_width // 2
    conv_width = conv_w.shape[2]
    n_main = 2 * qk_width + 2 * v_width
    gate_end = n_main + 2 * HEADS
    assert w_in.shape[2] == gate_end + 3 * conv_width + 2 * d
    assert conv_w.shape[1] == CONV_K and b_igate.shape[1] == HEADS and conv_width == v_width
    q_scale = float((qk_width // HEADS) ** -0.5)

    lane = jnp.arange(2 * LANES)[:, None] % LANES
    sel = (lane == jnp.arange(HEADS * LANES)[None, :] // LANES).astype(BF16)
    pad_lanes = lambda a: jnp.pad(a, ((0, 0), (0, LANES - a.shape[1])))

    x2d = x.reshape(bsz * s_len, d)
    for l in range(depth):
        w_main_t, w_gate_t = _wprep(jnp.swapaxes(w_in[l], 0, 1), n_main=n_main, gate_end=gate_end)
        w_gate_col = jnp.concatenate([pad_lanes(w_gate_t[:HEADS].T), pad_lanes(w_gate_t[HEADS:].T)], axis=1)
        bias_lane = jnp.concatenate([pad_lanes(b_igate[l][None, :]), pad_lanes(b_fgate[l][None, :])], axis=1)

        side = (w_branch_a[l], w_branch_b[l], w_out[l], w_gate[l], w_up[l], w_down[l])
        proj, gcol, (wa, wb, wo, wg, wu, wd) = _inproj(
            x2d, norm_mix[l].reshape(1, d), w_main_t, w_gate_col, side, qk_width=qk_width, q_scale=q_scale)
        hm = _mlstm(proj, gcol, bias_lane, mh_norm[l].reshape(1, v_width), sel,
                    bsz=bsz, s_len=s_len, qk_width=qk_width, v_width=v_width)
        x1, hf = _mix(hm, proj, conv_w[l], wa, wb, wo, x2d, norm_ffn[l].reshape(1, d),
                      s_len=s_len, col0=n_main)
        x2d = _ffn(hf, x1, wg, wu, wd, norm_final.reshape(1, d), final_norm=(l == depth - 1))
    return x2d.reshape(bsz, s_len, d)
```

```python
import functools

import jax
import jax.numpy as jnp
from jax import lax
from jax.experimental import pallas as pl
from jax.experimental.pallas import tpu as pltpu

F32 = jnp.float32
BF16 = jnp.bfloat16

HEADS = 8
CHUNK = 128
CONV_K = 3
EPS = 1e-6
LANES = 128
VMEM_LIMIT_BYTES = 60 * 1024 * 1024

NT_DIMS = (((1,), (1,)), ((), ()))
TN_DIMS = (((0,), (0,)), ((), ()))


def _params(semantics):
    return pltpu.CompilerParams(dimension_semantics=semantics,
                                vmem_limit_bytes=VMEM_LIMIT_BYTES)


def _rms_scale(x):
    return lax.rsqrt(jnp.mean(x * x, axis=-1, keepdims=True) + EPS)


def _wprep_kernel(wt_ref, main_ref, gate_ref, *, n_main, gate_end):
    main_ref[:n_main, :] = wt_ref[:n_main, :].astype(BF16)
    main_ref[n_main:, :] = wt_ref[gate_end:, :].astype(BF16)
    gate_ref[...] = wt_ref[n_main:gate_end, :].astype(BF16)


def _wprep(w_t, *, n_main, gate_end):
    n_in, d = w_t.shape
    n_gate = gate_end - n_main
    assert n_main % 16 == 0 and n_gate % 16 == 0 and d % LANES == 0
    kern = functools.partial(_wprep_kernel, n_main=n_main, gate_end=gate_end)
    return pl.pallas_call(
        kern,
        grid=(d // LANES,),
        in_specs=[pl.BlockSpec((n_in, LANES), lambda i: (0, i))],
        out_specs=[pl.BlockSpec((n_in - n_gate, LANES), lambda i: (0, i)),
                   pl.BlockSpec((n_gate, LANES), lambda i: (0, i))],
        out_shape=[jax.ShapeDtypeStruct((n_in - n_gate, d), BF16),
                   jax.ShapeDtypeStruct((n_gate, d), BF16)],
        compiler_params=_params(("parallel",)),
        name="wprep",
    )(w_t)


def _inproj_kernel(x_ref, g_ref, wt_ref, wgate_ref, *refs, q_width, q_scale, n_side):
    side_in, (proj_ref, gcol_ref), side_out, (h_ref,) = (
        refs[:n_side], refs[n_side:n_side + 2], refs[n_side + 2:2 * n_side + 2], refs[2 * n_side + 2:])
    j = pl.program_id(1)

    @pl.when(j == 0)
    def _():
        x = x_ref[...]
        h = (x * _rms_scale(x) * g_ref[...]).astype(BF16)
        h_ref[...] = h
        gcol_ref[...] = jnp.dot(h, wgate_ref[...], preferred_element_type=F32)

    acc = lax.dot_general(h_ref[...], wt_ref[...], NT_DIMS, preferred_element_type=F32)
    tn = acc.shape[1]
    col = j * tn + lax.broadcasted_iota(jnp.int32, (1, tn), 1)
    scale = jnp.where(col < q_width, q_scale, 1.0).astype(F32)
    proj_ref[...] = (acc * scale).astype(BF16)

    for w_ref, wb_ref in zip(side_in, side_out):
        wb_ref[...] = w_ref[...].astype(BF16)


def _side_chunk_rows(rows, n_steps):
    chunk = -(-rows // n_steps)
    chunk += -chunk % 16
    while rows % chunk:
        chunk += 16
    return chunk


def _inproj(x2d, g, w_main_t, w_gate, side_weights, *, qk_width, q_scale, tm=1024, tn=2048):
    t, d = x2d.shape
    n = w_main_t.shape[0]
    assert t % tm == 0 and n % tn == 0
    n_i, n_j = t // tm, n // tn
    kern = functools.partial(_inproj_kernel, q_width=qk_width, q_scale=q_scale, n_side=len(side_weights))
    x_map = lambda i, j: (jnp.where(j == 0, i, jnp.minimum(i + 1, n_i - 1)), 0)

    def side_spec(w):
        chunk = _side_chunk_rows(w.shape[0], n_i * n_j)
        last = w.shape[0] // chunk - 1
        return pl.BlockSpec((chunk, w.shape[1]), lambda i, j: (jnp.minimum(i * n_j + j, last), 0))

    side_specs = [side_spec(w) for w in side_weights]
    outs = pl.pallas_call(
        kern,
        grid=(n_i, n_j),
        in_specs=[
            pl.BlockSpec((tm, d), x_map),
            pl.BlockSpec((1, d), lambda i, j: (0, 0)),
            pl.BlockSpec((tn, d), lambda i, j: (j, 0)),
            pl.BlockSpec((d, 2 * LANES), lambda i, j: (0, 0)),
            *side_specs,
        ],
        out_specs=[
            pl.BlockSpec((tm, tn), lambda i, j: (i, j)),
            pl.BlockSpec((tm, 2 * LANES), lambda i, j: (i, 0)),
            *side_specs,
        ],
        out_shape=[
            jax.ShapeDtypeStruct((t, n), BF16),
            jax.ShapeDtypeStruct((t, 2 * LANES), F32),
            *[jax.ShapeDtypeStruct(w.shape, BF16) for w in side_weights],
        ],
        scratch_shapes=[pltpu.VMEM((tm, d), BF16)],
        compiler_params=_params(("arbitrary", "arbitrary")),
        name="inproj",
    )(x2d, g, w_main_t, w_gate, *side_weights)
    return outs[0], outs[1], outs[2:]


REP_D, REP_WK, REP_NM = range(3)
SEQS = 1
LOG2E = 1.4426950408889634


def _scan_rows(x, combine, identity):
    rows = lax.broadcasted_iota(jnp.int32, x.shape, 0)
    shift = 1
    while shift < x.shape[0]:
        x = combine(x, jnp.where(rows >= shift, pltpu.roll(x, shift, axis=0), identity))
        shift *= 2
    return x


def _mlstm_kernel(q_ref, k_ref, v_ref, o_ref, gcol_ref, bias_lane_ref, mh_ref, sel_ref,
                  out_ref, c_ref, n_ref, cn_ref, m_ref, s_ref, rep_ref, *, dk, dv):
    L = CHUNK

    @pl.when(pl.program_id(1) == 0)
    def _():
        c_ref[...] = jnp.zeros_like(c_ref)
        n_ref[...] = jnp.zeros_like(n_ref)
        cn_ref[...] = jnp.zeros_like(cn_ref)
        m_ref[...] = jnp.zeros_like(m_ref)

    seqs = range(SEQS)
    heads = range(HEADS)
    blk = lambda h: slice(h * LANES, (h + 1) * LANES)
    qk_cols = lambda h: slice(h * dk, (h + 1) * dk)
    v_cols = lambda h: slice(h * dv, (h + 1) * dv)

    for sq in seqs:
        for h in heads:
            s_ref[sq, h] = lax.dot_general(q_ref[sq, :, qk_cols(h)], k_ref[sq, :, qk_cols(h)], NT_DIMS,
                                           preferred_element_type=F32)

    row = lax.broadcasted_iota(jnp.int32, (L, L), 0)
    col = lax.broadcasted_iota(jnp.int32, (L, L), 1)
    causal = col <= row

    def replicate(sq, slot, xcol):
        hi = xcol.astype(BF16)
        lo = (xcol - hi.astype(F32)).astype(BF16)
        rep_ref[sq, slot] = jnp.dot(jnp.concatenate([hi, lo], axis=1), sel_ref[...],
                                    preferred_element_type=F32)

    crow, mrow, decay = [], [], []
    for sq in seqs:
        g2 = gcol_ref[sq] + bias_lane_ref[...]
        b_c = _scan_rows(jax.nn.log_sigmoid(g2[:, LANES:]), jnp.add, 0.0)
        c_c = g2[:, :LANES] - b_c
        m_prev = m_ref[sq, 0:1, :]
        u_c = jnp.maximum(_scan_rows(c_c, jnp.maximum, -jnp.inf), m_prev)
        r = u_c[L - 1:L, :]
        m_ref[sq] = jnp.broadcast_to(b_c[L - 1:L, :] + r, m_ref.shape[1:])
        replicate(sq, REP_D, (r - u_c) * LOG2E)
        replicate(sq, REP_WK, (c_c - r) * LOG2E)
        replicate(sq, REP_NM, -(b_c + u_c) * LOG2E)
        crow.append(((c_c - r) * LOG2E).T[:HEADS, :])
        mrow.append(jnp.broadcast_to((m_prev - r) * LOG2E, (L, LANES)).T[:HEADS, :])
        decay.append(jnp.exp2(mrow[sq]))

    ones = jnp.ones((L, LANES), BF16)
    for sq in seqs:
        for h in heads:
            d_h = rep_ref[sq, REP_D, :, blk(h)]
            w = jnp.exp2(jnp.where(causal, d_h + crow[sq][h:h + 1, :], -jnp.inf))
            inter = jnp.exp2(d_h + mrow[sq][h:h + 1, :])
            lhs = jnp.concatenate([s_ref[sq, h] * w, q_ref[sq, :, qk_cols(h)].astype(F32) * inter], axis=1)
            rhs = jnp.concatenate([jnp.concatenate([v_ref[sq, :, v_cols(h)], ones], axis=1), cn_ref[sq, h]],
                                  axis=0)
            res = jnp.dot(lhs.astype(BF16), rhs, preferred_element_type=F32)
            num = res[:, :dv]
            inv = 1.0 / jnp.maximum(jnp.abs(res[:, dv:]), jnp.exp2(rep_ref[sq, REP_NM, :, blk(h)]))
            ms = jnp.mean(num * num, axis=-1, keepdims=True)
            rr = inv * lax.rsqrt(ms * inv * inv + EPS)
            gate = jax.nn.sigmoid(o_ref[sq, :, v_cols(h)].astype(F32))
            hn = num * jnp.concatenate([rr, rr], axis=1) * mh_ref[:, v_cols(h)]
            out_ref[sq, :, v_cols(h)] = (gate * hn).astype(out_ref.dtype)

    for sq in seqs:
        for h in heads:
            wk = jnp.exp2(rep_ref[sq, REP_WK, :, blk(h)])
            kw = (k_ref[sq, :, qk_cols(h)].astype(F32) * wk).astype(BF16)
            v1 = jnp.concatenate([v_ref[sq, :, v_cols(h)], ones], axis=1)
            upd = lax.dot_general(kw, v1, TN_DIMS, preferred_element_type=F32)
            dec = decay[sq][h:h + 1, :]
            c_new = jnp.concatenate([dec, dec], axis=1) * c_ref[sq, h] + upd[:, :dv]
            n_new = dec * n_ref[sq, h] + upd[:, dv:]
            c_ref[sq, h] = c_new
            n_ref[sq, h] = n_new
            cn_ref[sq, h] = jnp.concatenate([c_new, n_new], axis=1).astype(BF16)


def _mlstm(proj, gcol, bias_lane, mh, sel, *, bsz, s_len, qk_width, v_width):
    nc = s_len // CHUNK
    dk, dv = qk_width // HEADS, v_width // HEADS
    assert v_width == 2 * qk_width and s_len % CHUNK == 0 and dk == CHUNK == LANES and dv == 2 * LANES
    assert bsz % SEQS == 0
    proj3 = proj.reshape(bsz, s_len, proj.shape[1])
    gcol3 = gcol.reshape(bsz, s_len, gcol.shape[1])
    step = lambda width, colblk: pl.BlockSpec((SEQS, CHUNK, width), lambda b, c: (b, c, colblk))
    const = lambda shape: pl.BlockSpec(shape, lambda b, c: (0, 0))
    kern = functools.partial(_mlstm_kernel, dk=dk, dv=dv)
    hm = pl.pallas_call(
        kern,
        grid=(bsz // SEQS, nc),
        in_specs=[
            step(qk_width, 0),
            step(qk_width, 1),
            step(v_width, 1),
            step(v_width, 2),
            step(2 * LANES, 0),
            const((1, 2 * LANES)),
            const((1, v_width)),
            const((2 * LANES, HEADS * LANES)),
        ],
        out_specs=step(v_width, 0),
        out_shape=jax.ShapeDtypeStruct((bsz, s_len, v_width), BF16),
        scratch_shapes=[
            pltpu.VMEM((SEQS, HEADS, dk, dv), F32),
            pltpu.VMEM((SEQS, HEADS, dk, LANES), F32),
            pltpu.VMEM((SEQS, HEADS, dk, dv + LANES), BF16),
            pltpu.VMEM((SEQS, 8, LANES), F32),
            pltpu.VMEM((SEQS, HEADS, CHUNK, CHUNK), F32),
            pltpu.VMEM((SEQS, 3, CHUNK, HEADS * LANES), F32),
        ],
        compiler_params=_params(("parallel", "arbitrary")),
        name="mlstm",
    )(proj3, proj3, proj3, proj3, gcol3, bias_lane, mh, sel)
    return hm.reshape(bsz * s_len, v_width)


HALO = 16


def _mix_kernel(hm_ref, cb_ref, cc_ref, cx_ref, cc_halo_ref, cx_halo_ref, cw_ref, ga_ref, gb_ref,
                wa_ref, wb_ref, wo_ref, x_ref, g_ref, x1_ref, hf_ref, cbu_ref, *, blocks_per_seq, lane_chunk):
    i = pl.program_id(0)
    tm, width = cbu_ref.shape

    ya = jnp.dot(hm_ref[...], wa_ref[...], preferred_element_type=F32)

    keep_halo = (i % blocks_per_seq != 0).astype(F32)
    rows = lax.broadcasted_iota(jnp.int32, (tm, lane_chunk), 0)
    for c0 in range(0, width, lane_chunk):
        sl = slice(c0, c0 + lane_chunk)
        p = cc_ref[:, sl].astype(F32) * cx_ref[:, sl].astype(F32)
        ph = (cc_halo_ref[:, sl].astype(F32) * cx_halo_ref[:, sl].astype(F32)) * keep_halo
        p1 = jnp.where(rows == 0, ph[HALO - 1:HALO, :], pltpu.roll(p, 1, axis=0))
        p2 = jnp.where(rows == 0, ph[HALO - 2:HALO - 1, :],
                       jnp.where(rows == 1, ph[HALO - 1:HALO, :], pltpu.roll(p, 2, axis=0)))
        u = cw_ref[0:1, sl] * p2 + cw_ref[1:2, sl] * p1 + cw_ref[2:3, sl] * p
        cbu_ref[:, sl] = (cb_ref[:, sl].astype(F32) * u).astype(BF16)

    yb = jnp.dot(cbu_ref[...], wb_ref[...], preferred_element_type=F32)
    merged = (jax.nn.sigmoid(ga_ref[...].astype(F32)) * ya
              + jax.nn.sigmoid(gb_ref[...].astype(F32)) * yb)
    x1 = x_ref[...] + jnp.dot(merged.astype(BF16), wo_ref[...], preferred_element_type=F32)
    x1_ref[...] = x1
    hf_ref[...] = (x1 * _rms_scale(x1) * g_ref[...]).astype(hf_ref.dtype)


def _mix(hm, proj, conv_w, wa, wb, wo, x2d, g, *, s_len, col0, tm=256):
    t, width = hm.shape
    d = x2d.shape[1]
    assert t % tm == 0 and s_len % tm == 0 and col0 % width == 0 and width == d
    assert wa.shape == wb.shape == wo.shape == (width, d)
    cblk = col0 // width
    halo_blocks = tm // HALO
    kern = functools.partial(_mix_kernel, blocks_per_seq=s_len // tm, lane_chunk=256)
    halo_map = lambda off: (lambda i: (jnp.maximum(i * halo_blocks - 1, 0), cblk + off))
    col_block = lambda c: pl.BlockSpec((tm, width), lambda i: (i, c))
    resident = lambda shape: pl.BlockSpec(shape, lambda i: (0, 0), pipeline_mode=pl.Buffered(1))
    return pl.pallas_call(
        kern,
        grid=(t // tm,),
        in_specs=[
            col_block(0),
            col_block(cblk),
            col_block(cblk + 1),
            col_block(cblk + 2),
            pl.BlockSpec((HALO, width), halo_map(1)),
            pl.BlockSpec((HALO, width), halo_map(2)),
            pl.BlockSpec((CONV_K, width), lambda i: (0, 0)),
            col_block(cblk + 3),
            col_block(cblk + 4),
            resident((width, d)),
            resident((width, d)),
            resident((width, d)),
            col_block(0),
            pl.BlockSpec((1, d), lambda i: (0, 0)),
        ],
        out_specs=[col_block(0), col_block(0)],
        out_shape=[jax.ShapeDtypeStruct((t, d), F32), jax.ShapeDtypeStruct((t, d), BF16)],
        scratch_shapes=[pltpu.VMEM((tm, width), BF16)],
        compiler_params=_params(("parallel",)),
        name="mix",
    )(hm, proj, proj, proj, proj, proj, conv_w, proj, proj, wa, wb, wo, x2d, g)


def _ffn_kernel(hf_ref, x1_hbm, wg_ref, wu_ref, wd_ref, gf_ref, out_ref, x1_buf, x1_sem,
                *, final_norm, epilogue_rows):
    i = pl.program_id(0)
    f = pl.program_id(1)
    tm = out_ref.shape[0]
    x1_copy = pltpu.make_async_copy(x1_hbm.at[pl.ds(pl.multiple_of(i * tm, tm), tm), :], x1_buf, x1_sem)

    @pl.when(f == 0)
    def _():
        x1_copy.start()
        out_ref[...] = jnp.zeros_like(out_ref)

    hf = hf_ref[...]
    gate = jnp.dot(hf, wg_ref[...], preferred_element_type=F32)
    up = jnp.dot(hf, wu_ref[...], preferred_element_type=F32)
    act = (gate * jax.nn.sigmoid(gate) * up).astype(BF16)
    out_ref[...] += jnp.dot(act, wd_ref[...], preferred_element_type=F32)

    @pl.when(f == pl.num_programs(1) - 1)
    def _():
        x1_copy.wait()
        for r0 in range(0, tm, epilogue_rows):
            rows = slice(r0, r0 + epilogue_rows)
            x2 = x1_buf[rows, :] + out_ref[rows, :]
            if final_norm:
                x2 = x2 * _rms_scale(x2) * gf_ref[...]
            out_ref[rows, :] = x2


def _ffn(hf, x1, wg, wu, wd, gf, *, final_norm, tm=1024, tf=512, epilogue_rows=32):
    t, d = x1.shape
    dff = wg.shape[1]
    assert t % tm == 0 and dff % tf == 0 and tm % epilogue_rows == 0
    kern = functools.partial(_ffn_kernel, final_norm=final_norm, epilogue_rows=epilogue_rows)
    return pl.pallas_call(
        kern,
        grid=(t // tm, dff // tf),
        in_specs=[
            pl.BlockSpec((tm, d), lambda i, f: (i, 0)),
            pl.BlockSpec(memory_space=pl.ANY),
            pl.BlockSpec((d, tf), lambda i, f: (0, f)),
            pl.BlockSpec((d, tf), lambda i, f: (0, f)),
            pl.BlockSpec((tf, d), lambda i, f: (f, 0)),
            pl.BlockSpec((1, d), lambda i, f: (0, 0)),
        ],
        out_specs=pl.BlockSpec((tm, d), lambda i, f: (i, 0)),
        out_shape=jax.ShapeDtypeStruct((t, d), F32),
        scratch_shapes=[pltpu.VMEM((tm, d), F32), pltpu.SemaphoreType.DMA(())],
        compiler_params=_params(("arbitrary", "arbitrary")),
        name="ffn",
    )(hf, x1, wg, wu, wd, gf)


def kernel(x, norm_mix, w_in, b_igate, b_fgate, conv_w, mh_norm, w_branch_a, w_branch_b,
           w_out, norm_ffn, w_gate, w_up, w_down, norm_final):
    bsz, s_len, d = x.shape
    depth = w_in.shape[0]
    v_width = mh_norm.shape[1]
    qk_width = v_width // 2
    conv_width = conv_w.shape[2]
    n_main = 2 * qk_width + 2 * v_width
    gate_end = n_main + 2 * HEADS
    assert w_in.shape[2] == gate_end + 3 * conv_width + 2 * d
    assert conv_w.shape[1] == CONV_K and b_igate.shape[1] == HEADS and conv_width == v_width
    q_scale = float((qk_width // HEADS) ** -0.5)

    lane = jnp.arange(2 * LANES)[:, None] % LANES
    sel = (lane == jnp.arange(HEADS * LANES)[None, :] // LANES).astype(BF16)
    pad_lanes = lambda a: jnp.pad(a, ((0, 0), (0, LANES - a.shape[1])))

    x2d = x.reshape(bsz * s_len, d)
    for l in range(depth):
        w_main_t, w_gate_t = _wprep(jnp.swapaxes(w_in[l], 0, 1), n_main=n_main, gate_end=gate_end)
        w_gate_col = jnp.concatenate([pad_lanes(w_gate_t[:HEADS].T), pad_lanes(w_gate_t[HEADS:].T)], axis=1)
        bias_lane = jnp.concatenate([pad_lanes(b_igate[l][None, :]), pad_lanes(b_fgate[l][None, :])], axis=1)

        side = (w_branch_a[l], w_branch_b[l], w_out[l], w_gate[l], w_up[l], w_down[l])
        proj, gcol, (wa, wb, wo, wg, wu, wd) = _inproj(
            x2d, norm_mix[l].reshape(1, d), w_main_t, w_gate_col, side, qk_width=qk_width, q_scale=q_scale)
        hm = _mlstm(proj, gcol, bias_lane, mh_norm[l].reshape(1, v_width), sel,
                    bsz=bsz, s_len=s_len, qk_width=qk_width, v_width=v_width)
        x1, hf = _mix(hm, proj, conv_w[l], wa, wb, wo, x2d, norm_ffn[l].reshape(1, d),
                      s_len=s_len, col0=n_main)
        x2d = _ffn(hf, x1, wg, wu, wd, norm_final.reshape(1, d), final_norm=(l == depth - 1))
    return x2d.reshape(bsz, s_len, d)
```

```python
import functools

import jax
import jax.numpy as jnp
from jax import lax
from jax.experimental import pallas as pl
from jax.experimental.pallas import tpu as pltpu

F32 = jnp.float32
BF16 = jnp.bfloat16

HEADS = 8
CHUNK = 128
CONV_K = 3
EPS = 1e-6
LANES = 128
VMEM_LIMIT_BYTES = 60 * 1024 * 1024

NT_DIMS = (((1,), (1,)), ((), ()))
TN_DIMS = (((0,), (0,)), ((), ()))


def _params(semantics):
    return pltpu.CompilerParams(dimension_semantics=semantics,
                                vmem_limit_bytes=VMEM_LIMIT_BYTES)


def _rms_scale(x):
    return lax.rsqrt(jnp.mean(x * x, axis=-1, keepdims=True) + EPS)


def _wprep_kernel(wt_ref, main_ref, gate_ref, *, n_main, gate_end):
    main_ref[:n_main, :] = wt_ref[:n_main, :].astype(BF16)
    main_ref[n_main:, :] = wt_ref[gate_end:, :].astype(BF16)
    gate_ref[...] = wt_ref[n_main:gate_end, :].astype(BF16)


def _wprep(w_t, *, n_main, gate_end):
    n_in, d = w_t.shape
    n_gate = gate_end - n_main
    assert n_main % 16 == 0 and n_gate % 16 == 0 and d % LANES == 0
    kern = functools.partial(_wprep_kernel, n_main=n_main, gate_end=gate_end)
    return pl.pallas_call(
        kern,
        grid=(d // LANES,),
        in_specs=[pl.BlockSpec((n_in, LANES), lambda i: (0, i))],
        out_specs=[pl.BlockSpec((n_in - n_gate, LANES), lambda i: (0, i)),
                   pl.BlockSpec((n_gate, LANES), lambda i: (0, i))],
        out_shape=[jax.ShapeDtypeStruct((n_in - n_gate, d), BF16),
                   jax.ShapeDtypeStruct((n_gate, d), BF16)],
        compiler_params=_params(("parallel",)),
        name="wprep",
    )(w_t)


def _inproj_kernel(x_ref, g_ref, wt_ref, wgate_ref, *refs, q_width, q_scale, n_side):
    side_in, (proj_ref, gcol_ref), side_out, (h_ref,) = (
        refs[:n_side], refs[n_side:n_side + 2], refs[n_side + 2:2 * n_side + 2], refs[2 * n_side + 2:])
    j = pl.program_id(1)

    @pl.when(j == 0)
    def _():
        x = x_ref[...]
        h = (x * _rms_scale(x) * g_ref[...]).astype(BF16)
        h_ref[...] = h
        gcol_ref[...] = jnp.dot(h, wgate_ref[...], preferred_element_type=F32)

    acc = lax.dot_general(h_ref[...], wt_ref[...], NT_DIMS, preferred_element_type=F32)
    tn = acc.shape[1]
    col = j * tn + lax.broadcasted_iota(jnp.int32, (1, tn), 1)
    scale = jnp.where(col < q_width, q_scale, 1.0).astype(F32)
    proj_ref[...] = (acc * scale).astype(BF16)

    for w_ref, wb_ref in zip(side_in, side_out):
        wb_ref[...] = w_ref[...].astype(BF16)


def _side_chunk_rows(rows, n_steps):
    chunk = -(-rows // n_steps)
    chunk += -chunk % 16
    while rows % chunk:
        chunk += 16
    return chunk


def _inproj(x2d, g, w_main_t, w_gate, side_weights, *, qk_width, q_scale, tm=1024, tn=2048):
    t, d = x2d.shape
    n = w_main_t.shape[0]
    assert t % tm == 0 and n % tn == 0
    n_i, n_j = t // tm, n // tn
    kern = functools.partial(_inproj_kernel, q_width=qk_width, q_scale=q_scale, n_side=len(side_weights))
    x_map = lambda i, j: (jnp.where(j == 0, i, jnp.minimum(i + 1, n_i - 1)), 0)

    def side_spec(w):
        chunk = _side_chunk_rows(w.shape[0], n_i * n_j)
        last = w.shape[0] // chunk - 1
        return pl.BlockSpec((chunk, w.shape[1]), lambda i, j: (jnp.minimum(i * n_j + j, last), 0))

    side_specs = [side_spec(w) for w in side_weights]
    outs = pl.pallas_call(
        kern,
        grid=(n_i, n_j),
        in_specs=[
            pl.BlockSpec((tm, d), x_map),
            pl.BlockSpec((1, d), lambda i, j: (0, 0)),
            pl.BlockSpec((tn, d), lambda i, j: (j, 0)),
            pl.BlockSpec((d, 2 * LANES), lambda i, j: (0, 0)),
            *side_specs,
        ],
        out_specs=[
            pl.BlockSpec((tm, tn), lambda i, j: (i, j)),
            pl.BlockSpec((tm, 2 * LANES), lambda i, j: (i, 0)),
            *side_specs,
        ],
        out_shape=[
            jax.ShapeDtypeStruct((t, n), BF16),
            jax.ShapeDtypeStruct((t, 2 * LANES), F32),
            *[jax.ShapeDtypeStruct(w.shape, BF16) for w in side_weights],
        ],
        scratch_shapes=[pltpu.VMEM((tm, d), BF16)],
        compiler_params=_params(("arbitrary", "arbitrary")),
        name="inproj",
    )(x2d, g, w_main_t, w_gate, *side_weights)
    return outs[0], outs[1], outs[2:]


REP_D, REP_WK, REP_NM = range(3)
SEQS = 1
LOG2E = 1.4426950408889634


def _scan_rows(x, combine, identity):
    rows = lax.broadcasted_iota(jnp.int32, x.shape, 0)
    shift = 1
    while shift < x.shape[0]:
        x = combine(x, jnp.where(rows >= shift, pltpu.roll(x, shift, axis=0), identity))
        shift *= 2
    return x


def _mlstm_kernel(q_ref, k_ref, v_ref, o_ref, gcol_ref, bias_lane_ref, mh_ref, sel_ref,
                  out_ref, c_ref, n_ref, cn_ref, m_ref, s_ref, rep_ref, *, dk, dv):
    L = CHUNK

    @pl.when(pl.program_id(1) == 0)
    def _():
        c_ref[...] = jnp.zeros_like(c_ref)
        n_ref[...] = jnp.zeros_like(n_ref)
        cn_ref[...] = jnp.zeros_like(cn_ref)
        m_ref[...] = jnp.zeros_like(m_ref)

    seqs = range(SEQS)
    heads = range(HEADS)
    blk = lambda h: slice(h * LANES, (h + 1) * LANES)
    qk_cols = lambda h: slice(h * dk, (h + 1) * dk)
    v_cols = lambda h: slice(h * dv, (h + 1) * dv)

    for sq in seqs:
        for h in heads:
            s_ref[sq, h] = lax.dot_general(q_ref[sq, :, qk_cols(h)], k_ref[sq, :, qk_cols(h)], NT_DIMS,
                                           preferred_element_type=F32)

    row = lax.broadcasted_iota(jnp.int32, (L, L), 0)
    col = lax.broadcasted_iota(jnp.int32, (L, L), 1)
    causal = col <= row

    def replicate(sq, slot, xcol):
        hi = xcol.astype(BF16)
        lo = (xcol - hi.astype(F32)).astype(BF16)
        rep_ref[sq, slot] = jnp.dot(jnp.concatenate([hi, lo], axis=1), sel_ref[...],
                                    preferred_element_type=F32)

    crow, mrow, decay = [], [], []
    for sq in seqs:
        g2 = gcol_ref[sq] + bias_lane_ref[...]
        b_c = _scan_rows(jax.nn.log_sigmoid(g2[:, LANES:]), jnp.add, 0.0)
        c_c = g2[:, :LANES] - b_c
        m_prev = m_ref[sq, 0:1, :]
        u_c = jnp.maximum(_scan_rows(c_c, jnp.maximum, -jnp.inf), m_prev)
        r = u_c[L - 1:L, :]
        m_ref[sq] = jnp.broadcast_to(b_c[L - 1:L, :] + r, m_ref.shape[1:])
        replicate(sq, REP_D, (r - u_c) * LOG2E)
        replicate(sq, REP_WK, (c_c - r) * LOG2E)
        replicate(sq, REP_NM, -(b_c + u_c) * LOG2E)
        crow.append(((c_c - r) * LOG2E).T[:HEADS, :])
        mrow.append(jnp.broadcast_to((m_prev - r) * LOG2E, (L, LANES)).T[:HEADS, :])
        decay.append(jnp.exp2(mrow[sq]))

    ones = jnp.ones((L, LANES), BF16)
    for sq in seqs:
        for h in heads:
            d_h = rep_ref[sq, REP_D, :, blk(h)]
            w = jnp.exp2(jnp.where(causal, d_h + crow[sq][h:h + 1, :], -jnp.inf))
            inter = jnp.exp2(d_h + mrow[sq][h:h + 1, :])
            lhs = jnp.concatenate([s_ref[sq, h] * w, q_ref[sq, :, qk_cols(h)].astype(F32) * inter], axis=1)
            rhs = jnp.concatenate([jnp.concatenate([v_ref[sq, :, v_cols(h)], ones], axis=1), cn_ref[sq, h]],
                                  axis=0)
            res = jnp.dot(lhs.astype(BF16), rhs, preferred_element_type=F32)
            num = res[:, :dv]
            inv = 1.0 / jnp.maximum(jnp.abs(res[:, dv:]), jnp.exp2(rep_ref[sq, REP_NM, :, blk(h)]))
            ms = jnp.mean(num * num, axis=-1, keepdims=True)
            rr = inv * lax.rsqrt(ms * inv * inv + EPS)
            gate = jax.nn.sigmoid(o_ref[sq, :, v_cols(h)].astype(F32))
            hn = num * jnp.concatenate([rr, rr], axis=1) * mh_ref[:, v_cols(h)]
            out_ref[sq, :, v_cols(h)] = (gate * hn).astype(out_ref.dtype)

    for sq in seqs:
        for h in heads:
            wk = jnp.exp2(rep_ref[sq, REP_WK, :, blk(h)])
            kw = (k_ref[sq, :, qk_cols(h)].astype(F32) * wk).astype(BF16)
            v1 = jnp.concatenate([v_ref[sq, :, v_cols(h)], ones], axis=1)
            upd = lax.dot_general(kw, v1, TN_DIMS, preferred_element_type=F32)
            dec = decay[sq][h:h + 1, :]
            c_new = jnp.concatenate([dec, dec], axis=1) * c_ref[sq, h] + upd[:, :dv]
            n_new = dec * n_ref[sq, h] + upd[:, dv:]
            c_ref[sq, h] = c_new
            n_ref[sq, h] = n_new
            cn_ref[sq, h] = jnp.concatenate([c_new, n_new], axis=1).astype(BF16)


def _mlstm(proj, gcol, bias_lane, mh, sel, *, bsz, s_len, qk_width, v_width):
    nc = s_len // CHUNK
    dk, dv = qk_width // HEADS, v_width // HEADS
    assert v_width == 2 * qk_width and s_len % CHUNK == 0 and dk == CHUNK == LANES and dv == 2 * LANES
    assert bsz % SEQS == 0
    proj3 = proj.reshape(bsz, s_len, proj.shape[1])
    gcol3 = gcol.reshape(bsz, s_len, gcol.shape[1])
    step = lambda width, colblk: pl.BlockSpec((SEQS, CHUNK, width), lambda b, c: (b, c, colblk))
    const = lambda shape: pl.BlockSpec(shape, lambda b, c: (0, 0))
    kern = functools.partial(_mlstm_kernel, dk=dk, dv=dv)
    hm = pl.pallas_call(
        kern,
        grid=(bsz // SEQS, nc),
        in_specs=[
            step(qk_width, 0),
            step(qk_width, 1),
            step(v_width, 1),
            step(v_width, 2),
            step(2 * LANES, 0),
            const((1, 2 * LANES)),
            const((1, v_width)),
            const((2 * LANES, HEADS * LANES)),
        ],
        out_specs=step(v_width, 0),
        out_shape=jax.ShapeDtypeStruct((bsz, s_len, v_width), BF16),
        scratch_shapes=[
            pltpu.VMEM((SEQS, HEADS, dk, dv), F32),
            pltpu.VMEM((SEQS, HEADS, dk, LANES), F32),
            pltpu.VMEM((SEQS, HEADS, dk, dv + LANES), BF16),
            pltpu.VMEM((SEQS, 8, LANES), F32),
            pltpu.VMEM((SEQS, HEADS, CHUNK, CHUNK), F32),
            pltpu.VMEM((SEQS, 3, CHUNK, HEADS * LANES), F32),
        ],
        compiler_params=_params(("parallel", "arbitrary")),
        name="mlstm",
    )(proj3, proj3, proj3, proj3, gcol3, bias_lane, mh, sel)
    return hm.reshape(bsz * s_len, v_width)


HALO = 16


def _mix_kernel(hm_ref, cb_ref, cc_ref, cx_ref, cc_halo_ref, cx_halo_ref, cw_ref, ga_ref, gb_ref,
                wa_ref, wb_ref, wo_ref, x_ref, g_ref, x1_ref, hf_ref, *, blocks_per_seq, lane_chunk):
    i = pl.program_id(0)
    tm, width = hm_ref.shape

    ya = jnp.dot(hm_ref[...], wa_ref[...], preferred_element_type=F32)

    keep_halo = (i % blocks_per_seq != 0).astype(F32)
    rows = lax.broadcasted_iota(jnp.int32, (tm, lane_chunk), 0)
    for c0 in range(0, width, lane_chunk):
        sl = slice(c0, c0 + lane_chunk)
        p = cc_ref[:, sl].astype(F32) * cx_ref[:, sl].astype(F32)
        ph = (cc_halo_ref[:, sl].astype(F32) * cx_halo_ref[:, sl].astype(F32)) * keep_halo
        p1 = jnp.where(rows == 0, ph[HALO - 1:HALO, :], pltpu.roll(p, 1, axis=0))
        p2 = jnp.where(rows == 0, ph[HALO - 2:HALO - 1, :],
                       jnp.where(rows == 1, ph[HALO - 1:HALO, :], pltpu.roll(p, 2, axis=0)))
        u = cw_ref[0:1, sl] * p2 + cw_ref[1:2, sl] * p1 + cw_ref[2:3, sl] * p
        cbu = (cb_ref[:, sl].astype(F32) * u).astype(BF16)
        part = jnp.dot(cbu, wb_ref[sl, :], preferred_element_type=F32)
        yb = part if c0 == 0 else yb + part
    merged = (jax.nn.sigmoid(ga_ref[...].astype(F32)) * ya
              + jax.nn.sigmoid(gb_ref[...].astype(F32)) * yb)
    x1 = x_ref[...] + jnp.dot(merged.astype(BF16), wo_ref[...], preferred_element_type=F32)
    x1_ref[...] = x1
    hf_ref[...] = (x1 * _rms_scale(x1) * g_ref[...]).astype(hf_ref.dtype)


def _mix(hm, proj, conv_w, wa, wb, wo, x2d, g, *, s_len, col0, tm=256):
    t, width = hm.shape
    d = x2d.shape[1]
    assert t % tm == 0 and s_len % tm == 0 and col0 % width == 0 and width == d
    assert wa.shape == wb.shape == wo.shape == (width, d)
    cblk = col0 // width
    halo_blocks = tm // HALO
    kern = functools.partial(_mix_kernel, blocks_per_seq=s_len // tm, lane_chunk=256)
    halo_map = lambda off: (lambda i: (jnp.maximum(i * halo_blocks - 1, 0), cblk + off))
    col_block = lambda c: pl.BlockSpec((tm, width), lambda i: (i, c))
    resident = lambda shape: pl.BlockSpec(shape, lambda i: (0, 0), pipeline_mode=pl.Buffered(1))
    return pl.pallas_call(
        kern,
        grid=(t // tm,),
        in_specs=[
            col_block(0),
            col_block(cblk),
            col_block(cblk + 1),
            col_block(cblk + 2),
            pl.BlockSpec((HALO, width), halo_map(1)),
            pl.BlockSpec((HALO, width), halo_map(2)),
            pl.BlockSpec((CONV_K, width), lambda i: (0, 0)),
            col_block(cblk + 3),
            col_block(cblk + 4),
            resident((width, d)),
            resident((width, d)),
            resident((width, d)),
            col_block(0),
            pl.BlockSpec((1, d), lambda i: (0, 0)),
        ],
        out_specs=[col_block(0), col_block(0)],
        out_shape=[jax.ShapeDtypeStruct((t, d), F32), jax.ShapeDtypeStruct((t, d), BF16)],
        compiler_params=_params(("parallel",)),
        name="mix",
    )(hm, proj, proj, proj, proj, proj, conv_w, proj, proj, wa, wb, wo, x2d, g)


def _ffn_kernel(hf_ref, x1_hbm, wg_ref, wu_ref, wd_ref, gf_ref, out_ref, x1_buf, x1_sem,
                *, final_norm, epilogue_rows):
    i = pl.program_id(0)
    f = pl.program_id(1)
    tm = out_ref.shape[0]
    x1_copy = pltpu.make_async_copy(x1_hbm.at[pl.ds(pl.multiple_of(i * tm, tm), tm), :], x1_buf, x1_sem)

    @pl.when(f == 0)
    def _():
        x1_copy.start()
        out_ref[...] = jnp.zeros_like(out_ref)

    hf = hf_ref[...]
    gate = jnp.dot(hf, wg_ref[...], preferred_element_type=F32)
    up = jnp.dot(hf, wu_ref[...], preferred_element_type=F32)
    act = (gate * jax.nn.sigmoid(gate) * up).astype(BF16)
    out_ref[...] += jnp.dot(act, wd_ref[...], preferred_element_type=F32)

    @pl.when(f == pl.num_programs(1) - 1)
    def _():
        x1_copy.wait()
        for r0 in range(0, tm, epilogue_rows):
            rows = slice(r0, r0 + epilogue_rows)
            x2 = x1_buf[rows, :] + out_ref[rows, :]
            if final_norm:
                x2 = x2 * _rms_scale(x2) * gf_ref[...]
            out_ref[rows, :] = x2


def _ffn(hf, x1, wg, wu, wd, gf, *, final_norm, tm=1024, tf=512, epilogue_rows=32):
    t, d = x1.shape
    dff = wg.shape[1]
    assert t % tm == 0 and dff % tf == 0 and tm % epilogue_rows == 0
    kern = functools.partial(_ffn_kernel, final_norm=final_norm, epilogue_rows=epilogue_rows)
    return pl.pallas_call(
        kern,
        grid=(t // tm, dff // tf),
        in_specs=[
            pl.BlockSpec((tm, d), lambda i, f: (i, 0)),
            pl.BlockSpec(memory_space=pl.ANY),
            pl.BlockSpec((d, tf), lambda i, f: (0, f)),
            pl.BlockSpec((d, tf), lambda i, f: (0, f)),
            pl.BlockSpec((tf, d), lambda i, f: (f, 0)),
            pl.BlockSpec((1, d), lambda i, f: (0, 0)),
        ],
        out_specs=pl.BlockSpec((tm, d), lambda i, f: (i, 0)),
        out_shape=jax.ShapeDtypeStruct((t, d), F32),
        scratch_shapes=[pltpu.VMEM((tm, d), F32), pltpu.SemaphoreType.DMA(())],
        compiler_params=_params(("arbitrary", "arbitrary")),
        name="ffn",
    )(hf, x1, wg, wu, wd, gf)


def kernel(x, norm_mix, w_in, b_igate, b_fgate, conv_w, mh_norm, w_branch_a, w_branch_b,
           w_out, norm_ffn, w_gate, w_up, w_down, norm_final):
    bsz, s_len, d = x.shape
    depth = w_in.shape[0]
    v_width = mh_norm.shape[1]
    qk_width = v_width // 2
    conv_width = conv_w.shape[2]
    n_main = 2 * qk_width + 2 * v_width
    gate_end = n_main + 2 * HEADS
    assert w_in.shape[2] == gate_end + 3 * conv_width + 2 * d
    assert conv_w.shape[1] == CONV_K and b_igate.shape[1] == HEADS and conv_width == v_width
    q_scale = float((qk_width // HEADS) ** -0.5)

    lane = jnp.arange(2 * LANES)[:, None] % LANES
    sel = (lane == jnp.arange(HEADS * LANES)[None, :] // LANES).astype(BF16)
    pad_lanes = lambda a: jnp.pad(a, ((0, 0), (0, LANES - a.shape[1])))

    x2d = x.reshape(bsz * s_len, d)
    for l in range(depth):
        w_main_t, w_gate_t = _wprep(jnp.swapaxes(w_in[l], 0, 1), n_main=n_main, gate_end=gate_end)
        w_gate_col = jnp.concatenate([pad_lanes(w_gate_t[:HEADS].T), pad_lanes(w_gate_t[HEADS:].T)], axis=1)
        bias_lane = jnp.concatenate([pad_lanes(b_igate[l][None, :]), pad_lanes(b_fgate[l][None, :])], axis=1)

        side = (w_branch_a[l], w_branch_b[l], w_out[l], w_gate[l], w_up[l], w_down[l])
        proj, gcol, (wa, wb, wo, wg, wu, wd) = _inproj(
            x2d, norm_mix[l].reshape(1, d), w_main_t, w_gate_col, side, qk_width=qk_width, q_scale=q_scale)
        hm = _mlstm(proj, gcol, bias_lane, mh_norm[l].reshape(1, v_width), sel,
                    bsz=bsz, s_len=s_len, qk_width=qk_width, v_width=v_width)
        x1, hf = _mix(hm, proj, conv_w[l], wa, wb, wo, x2d, norm_ffn[l].reshape(1, d),
                      s_len=s_len, col0=n_main)
        x2d = _ffn(hf, x1, wg, wu, wd, norm_final.reshape(1, d), final_norm=(l == depth - 1))
    return x2d.reshape(bsz, s_len, d)
```

```python
import functools

import jax
import jax.numpy as jnp
from jax import lax
from jax.experimental import pallas as pl
from jax.experimental.pallas import tpu as pltpu

F32 = jnp.float32
BF16 = jnp.bfloat16

HEADS = 8
CHUNK = 128
CONV_K = 3
EPS = 1e-6
LANES = 128
VMEM_LIMIT_BYTES = 60 * 1024 * 1024

NT_DIMS = (((1,), (1,)), ((), ()))
TN_DIMS = (((0,), (0,)), ((), ()))


def _params(semantics):
    return pltpu.CompilerParams(dimension_semantics=semantics,
                                vmem_limit_bytes=VMEM_LIMIT_BYTES)


def _rms_scale(x):
    return lax.rsqrt(jnp.mean(x * x, axis=-1, keepdims=True) + EPS)


def _wprep_kernel(wt_ref, main_ref, gate_ref, *, n_main, gate_end):
    main_ref[:n_main, :] = wt_ref[:n_main, :].astype(BF16)
    main_ref[n_main:, :] = wt_ref[gate_end:, :].astype(BF16)
    gate_ref[...] = wt_ref[n_main:gate_end, :].astype(BF16)


def _wprep(w_t, *, n_main, gate_end):
    n_in, d = w_t.shape
    n_gate = gate_end - n_main
    assert n_main % 16 == 0 and n_gate % 16 == 0 and d % LANES == 0
    kern = functools.partial(_wprep_kernel, n_main=n_main, gate_end=gate_end)
    return pl.pallas_call(
        kern,
        grid=(d // LANES,),
        in_specs=[pl.BlockSpec((n_in, LANES), lambda i: (0, i))],
        out_specs=[pl.BlockSpec((n_in - n_gate, LANES), lambda i: (0, i)),
                   pl.BlockSpec((n_gate, LANES), lambda i: (0, i))],
        out_shape=[jax.ShapeDtypeStruct((n_in - n_gate, d), BF16),
                   jax.ShapeDtypeStruct((n_gate, d), BF16)],
        compiler_params=_params(("parallel",)),
        name="wprep",
    )(w_t)


def _inproj_kernel(x_ref, g_ref, wt_ref, wgate_ref, *refs, q_width, q_scale, n_side):
    side_in, (proj_ref, gcol_ref), side_out, (h_ref,) = (
        refs[:n_side], refs[n_side:n_side + 2], refs[n_side + 2:2 * n_side + 2], refs[2 * n_side + 2:])
    j = pl.program_id(1)

    @pl.when(j == 0)
    def _():
        x = x_ref[...]
        h = (x * _rms_scale(x) * g_ref[...]).astype(BF16)
        h_ref[...] = h
        gcol_ref[...] = jnp.dot(h, wgate_ref[...], preferred_element_type=F32)

    acc = lax.dot_general(h_ref[...], wt_ref[...], NT_DIMS, preferred_element_type=F32)
    tn = acc.shape[1]
    col = j * tn + lax.broadcasted_iota(jnp.int32, (1, tn), 1)
    scale = jnp.where(col < q_width, q_scale, 1.0).astype(F32)
    proj_ref[...] = (acc * scale).astype(BF16)

    for w_ref, wb_ref in zip(side_in, side_out):
        wb_ref[...] = w_ref[...].astype(BF16)


def _side_chunk_rows(rows, n_steps):
    chunk = -(-rows // n_steps)
    chunk += -chunk % 16
    while rows % chunk:
        chunk += 16
    return chunk


def _inproj(x2d, g, w_main_t, w_gate, side_weights, *, qk_width, q_scale, tm=1024, tn=2048):
    t, d = x2d.shape
    n = w_main_t.shape[0]
    assert t % tm == 0 and n % tn == 0
    n_i, n_j = t // tm, n // tn
    kern = functools.partial(_inproj_kernel, q_width=qk_width, q_scale=q_scale, n_side=len(side_weights))
    x_map = lambda i, j: (jnp.where(j == 0, i, jnp.minimum(i + 1, n_i - 1)), 0)

    def side_spec(w):
        chunk = _side_chunk_rows(w.shape[0], n_i * n_j)
        last = w.shape[0] // chunk - 1
        return pl.BlockSpec((chunk, w.shape[1]), lambda i, j: (jnp.minimum(i * n_j + j, last), 0))

    side_specs = [side_spec(w) for w in side_weights]
    outs = pl.pallas_call(
        kern,
        grid=(n_i, n_j),
        in_specs=[
            pl.BlockSpec((tm, d), x_map),
            pl.BlockSpec((1, d), lambda i, j: (0, 0)),
            pl.BlockSpec((tn, d), lambda i, j: (j, 0)),
            pl.BlockSpec((d, 2 * LANES), lambda i, j: (0, 0)),
            *side_specs,
        ],
        out_specs=[
            pl.BlockSpec((tm, tn), lambda i, j: (i, j)),
            pl.BlockSpec((tm, 2 * LANES), lambda i, j: (i, 0)),
            *side_specs,
        ],
        out_shape=[
            jax.ShapeDtypeStruct((t, n), BF16),
            jax.ShapeDtypeStruct((t, 2 * LANES), F32),
            *[jax.ShapeDtypeStruct(w.shape, BF16) for w in side_weights],
        ],
        scratch_shapes=[pltpu.VMEM((tm, d), BF16)],
        compiler_params=_params(("arbitrary", "arbitrary")),
        name="inproj",
    )(x2d, g, w_main_t, w_gate, *side_weights)
    return outs[0], outs[1], outs[2:]


REP_D, REP_WK, REP_NM = range(3)
SEQS = 1
LOG2E = 1.4426950408889634


def _scan_rows(x, combine, identity):
    rows = lax.broadcasted_iota(jnp.int32, x.shape, 0)
    shift = 1
    while shift < x.shape[0]:
        x = combine(x, jnp.where(rows >= shift, pltpu.roll(x, shift, axis=0), identity))
        shift *= 2
    return x


def _mlstm_kernel(qkvo_ref, gcol_ref, bias_lane_ref, mh_ref, sel_ref,
                  out_ref, c_ref, n_ref, cn_ref, m_ref, s_ref, rep_ref, *, dk, dv):
    L = CHUNK

    @pl.when(pl.program_id(1) == 0)
    def _():
        c_ref[...] = jnp.zeros_like(c_ref)
        n_ref[...] = jnp.zeros_like(n_ref)
        cn_ref[...] = jnp.zeros_like(cn_ref)
        m_ref[...] = jnp.zeros_like(m_ref)

    seqs = range(SEQS)
    heads = range(HEADS)
    blk = lambda h: slice(h * LANES, (h + 1) * LANES)
    v_cols = lambda h: slice(h * dv, (h + 1) * dv)
    q_of = lambda sq, h: qkvo_ref[sq, :, h * dk:(h + 1) * dk]
    k_of = lambda sq, h: qkvo_ref[sq, :, HEADS * dk + h * dk:HEADS * dk + (h + 1) * dk]
    v_of = lambda sq, h: qkvo_ref[sq, :, 2 * HEADS * dk + h * dv:2 * HEADS * dk + (h + 1) * dv]
    o_of = lambda sq, h: qkvo_ref[sq, :, HEADS * (2 * dk + dv) + h * dv:HEADS * (2 * dk + dv) + (h + 1) * dv]

    for sq in seqs:
        for h in heads:
            s_ref[sq, h] = lax.dot_general(q_of(sq, h), k_of(sq, h), NT_DIMS,
                                           preferred_element_type=F32)

    row = lax.broadcasted_iota(jnp.int32, (L, L), 0)
    col = lax.broadcasted_iota(jnp.int32, (L, L), 1)
    causal = col <= row

    def replicate(sq, slot, xcol):
        hi = xcol.astype(BF16)
        lo = (xcol - hi.astype(F32)).astype(BF16)
        rep_ref[sq, slot] = jnp.dot(jnp.concatenate([hi, lo], axis=1), sel_ref[...],
                                    preferred_element_type=F32)

    crow, mrow, decay = [], [], []
    for sq in seqs:
        g2 = gcol_ref[sq] + bias_lane_ref[...]
        b_c = _scan_rows(jax.nn.log_sigmoid(g2[:, LANES:]), jnp.add, 0.0)
        c_c = g2[:, :LANES] - b_c
        m_prev = m_ref[sq, 0:1, :]
        u_c = jnp.maximum(_scan_rows(c_c, jnp.maximum, -jnp.inf), m_prev)
        r = u_c[L - 1:L, :]
        m_ref[sq] = jnp.broadcast_to(b_c[L - 1:L, :] + r, m_ref.shape[1:])
        replicate(sq, REP_D, (r - u_c) * LOG2E)
        replicate(sq, REP_WK, (c_c - r) * LOG2E)
        replicate(sq, REP_NM, -(b_c + u_c) * LOG2E)
        crow.append(((c_c - r) * LOG2E).T[:HEADS, :])
        mrow.append(jnp.broadcast_to((m_prev - r) * LOG2E, (L, LANES)).T[:HEADS, :])
        decay.append(jnp.exp2(mrow[sq]))

    ones = jnp.ones((L, LANES), BF16)
    for sq in seqs:
        for h in heads:
            d_h = rep_ref[sq, REP_D, :, blk(h)]
            w = jnp.exp2(jnp.where(causal, d_h + crow[sq][h:h + 1, :], -jnp.inf))
            inter = jnp.exp2(d_h + mrow[sq][h:h + 1, :])
            lhs = jnp.concatenate([s_ref[sq, h] * w, q_of(sq, h).astype(F32) * inter], axis=1)
            rhs = jnp.concatenate([jnp.concatenate([v_of(sq, h), ones], axis=1), cn_ref[sq, h]], axis=0)
            res = jnp.dot(lhs.astype(BF16), rhs, preferred_element_type=F32)
            num = res[:, :dv]
            inv = 1.0 / jnp.maximum(jnp.abs(res[:, dv:]), jnp.exp2(rep_ref[sq, REP_NM, :, blk(h)]))
            ms = jnp.mean(num * num, axis=-1, keepdims=True)
            rr = inv * lax.rsqrt(ms * inv * inv + EPS)
            gate = jax.nn.sigmoid(o_of(sq, h).astype(F32))
            hn = num * jnp.concatenate([rr, rr], axis=1) * mh_ref[:, v_cols(h)]
            out_ref[sq, :, v_cols(h)] = (gate * hn).astype(out_ref.dtype)

    for sq in seqs:
        for h in heads:
            wk = jnp.exp2(rep_ref[sq, REP_WK, :, blk(h)])
            kw = (k_of(sq, h).astype(F32) * wk).astype(BF16)
            v1 = jnp.concatenate([v_of(sq, h), ones], axis=1)
            upd = lax.dot_general(kw, v1, TN_DIMS, preferred_element_type=F32)
            dec = decay[sq][h:h + 1, :]
            c_new = jnp.concatenate([dec, dec], axis=1) * c_ref[sq, h] + upd[:, :dv]
            n_new = dec * n_ref[sq, h] + upd[:, dv:]
            c_ref[sq, h] = c_new
            n_ref[sq, h] = n_new
            cn_ref[sq, h] = jnp.concatenate([c_new, n_new], axis=1).astype(BF16)


def _mlstm(proj, gcol, bias_lane, mh, sel, *, bsz, s_len, qk_width, v_width):
    nc = s_len // CHUNK
    dk, dv = qk_width // HEADS, v_width // HEADS
    assert v_width == 2 * qk_width and s_len % CHUNK == 0 and dk == CHUNK == LANES and dv == 2 * LANES
    assert bsz % SEQS == 0
    proj3 = proj.reshape(bsz, s_len, proj.shape[1])
    gcol3 = gcol.reshape(bsz, s_len, gcol.shape[1])
    step = lambda width, colblk: pl.BlockSpec((SEQS, CHUNK, width), lambda b, c: (b, c, colblk))
    const = lambda shape: pl.BlockSpec(shape, lambda b, c: (0, 0))
    kern = functools.partial(_mlstm_kernel, dk=dk, dv=dv)
    hm = pl.pallas_call(
        kern,
        grid=(bsz // SEQS, nc),
        in_specs=[
            step(2 * qk_width + 2 * v_width, 0),
            step(2 * LANES, 0),
            const((1, 2 * LANES)),
            const((1, v_width)),
            const((2 * LANES, HEADS * LANES)),
        ],
        out_specs=step(v_width, 0),
        out_shape=jax.ShapeDtypeStruct((bsz, s_len, v_width), BF16),
        scratch_shapes=[
            pltpu.VMEM((SEQS, HEADS, dk, dv), F32),
            pltpu.VMEM((SEQS, HEADS, dk, LANES), F32),
            pltpu.VMEM((SEQS, HEADS, dk, dv + LANES), BF16),
            pltpu.VMEM((SEQS, 8, LANES), F32),
            pltpu.VMEM((SEQS, HEADS, CHUNK, CHUNK), F32),
            pltpu.VMEM((SEQS, 3, CHUNK, HEADS * LANES), F32),
        ],
        compiler_params=_params(("parallel", "arbitrary")),
        name="mlstm",
    )(proj3, gcol3, bias_lane, mh, sel)
    return hm.reshape(bsz * s_len, v_width)


HALO = 16


def _mix_kernel(hm_ref, conv_ref, halo_ref, cw_ref, gab_ref, wa_ref, wb_ref, wo_ref, x_ref, g_ref,
                x1_ref, hf_ref, cbu_ref, *, blocks_per_seq, lane_chunk):
    i = pl.program_id(0)
    tm, width = cbu_ref.shape

    ya = jnp.dot(hm_ref[...], wa_ref[...], preferred_element_type=F32)

    keep_halo = (i % blocks_per_seq != 0).astype(F32)
    rows = lax.broadcasted_iota(jnp.int32, (tm, lane_chunk), 0)
    for c0 in range(0, width, lane_chunk):
        sl = slice(c0, c0 + lane_chunk)
        sl_cc, sl_cx = slice(width + c0, width + c0 + lane_chunk), slice(2 * width + c0, 2 * width + c0 + lane_chunk)
        p = conv_ref[:, sl_cc].astype(F32) * conv_ref[:, sl_cx].astype(F32)
        ph = (halo_ref[:, sl].astype(F32)
              * halo_ref[:, width + c0:width + c0 + lane_chunk].astype(F32)) * keep_halo
        p1 = jnp.where(rows == 0, ph[HALO - 1:HALO, :], pltpu.roll(p, 1, axis=0))
        p2 = jnp.where(rows == 0, ph[HALO - 2:HALO - 1, :],
                       jnp.where(rows == 1, ph[HALO - 1:HALO, :], pltpu.roll(p, 2, axis=0)))
        u = cw_ref[0:1, sl] * p2 + cw_ref[1:2, sl] * p1 + cw_ref[2:3, sl] * p
        cbu_ref[:, sl] = (conv_ref[:, sl].astype(F32) * u).astype(BF16)

    yb = jnp.dot(cbu_ref[...], wb_ref[...], preferred_element_type=F32)
    merged = (jax.nn.sigmoid(gab_ref[:, :width].astype(F32)) * ya
              + jax.nn.sigmoid(gab_ref[:, width:].astype(F32)) * yb)
    x1 = x_ref[...] + jnp.dot(merged.astype(BF16), wo_ref[...], preferred_element_type=F32)
    x1_ref[...] = x1
    hf_ref[...] = (x1 * _rms_scale(x1) * g_ref[...]).astype(hf_ref.dtype)


def _mix(hm, proj, conv_w, wa, wb, wo, x2d, g, *, s_len, col0, tm=256):
    t, width = hm.shape
    d = x2d.shape[1]
    assert t % tm == 0 and s_len % tm == 0 and width == d
    assert wa.shape == wb.shape == wo.shape == (width, d)
    assert col0 % (3 * width) == 0 and (col0 + width) % (2 * width) == 0
    halo_blocks = tm // HALO
    kern = functools.partial(_mix_kernel, blocks_per_seq=s_len // tm, lane_chunk=256)
    col_block = lambda c: pl.BlockSpec((tm, width), lambda i: (i, c))
    resident = lambda shape: pl.BlockSpec(shape, lambda i: (0, 0), pipeline_mode=pl.Buffered(1))
    return pl.pallas_call(
        kern,
        grid=(t // tm,),
        in_specs=[
            col_block(0),
            pl.BlockSpec((tm, 3 * width), lambda i: (i, col0 // (3 * width))),
            pl.BlockSpec((HALO, 2 * width),
                         lambda i: (jnp.maximum(i * halo_blocks - 1, 0), (col0 + width) // (2 * width))),
            pl.BlockSpec((CONV_K, width), lambda i: (0, 0)),
            pl.BlockSpec((tm, 2 * width), lambda i: (i, (col0 + 3 * width) // (2 * width))),
            resident((width, d)),
            resident((width, d)),
            resident((width, d)),
            col_block(0),
            pl.BlockSpec((1, d), lambda i: (0, 0)),
        ],
        out_specs=[col_block(0), col_block(0)],
        out_shape=[jax.ShapeDtypeStruct((t, d), F32), jax.ShapeDtypeStruct((t, d), BF16)],
        scratch_shapes=[pltpu.VMEM((tm, width), BF16)],
        compiler_params=_params(("parallel",)),
        name="mix",
    )(hm, proj, proj, conv_w, proj, wa, wb, wo, x2d, g)


def _ffn_kernel(hf_ref, x1_hbm, wg_ref, wu_ref, wd_ref, gf_ref, out_ref, x1_buf, x1_sem,
                *, final_norm, epilogue_rows):
    i = pl.program_id(0)
    f = pl.program_id(1)
    tm = out_ref.shape[0]
    x1_copy = pltpu.make_async_copy(x1_hbm.at[pl.ds(pl.multiple_of(i * tm, tm), tm), :], x1_buf, x1_sem)

    @pl.when(f == 0)
    def _():
        x1_copy.start()
        out_ref[...] = jnp.zeros_like(out_ref)

    hf = hf_ref[...]
    gate = jnp.dot(hf, wg_ref[...], preferred_element_type=F32)
    up = jnp.dot(hf, wu_ref[...], preferred_element_type=F32)
    act = (gate * jax.nn.sigmoid(gate) * up).astype(BF16)
    out_ref[...] += jnp.dot(act, wd_ref[...], preferred_element_type=F32)

    @pl.when(f == pl.num_programs(1) - 1)
    def _():
        x1_copy.wait()
        for r0 in range(0, tm, epilogue_rows):
            rows = slice(r0, r0 + epilogue_rows)
            x2 = x1_buf[rows, :] + out_ref[rows, :]
            if final_norm:
                x2 = x2 * _rms_scale(x2) * gf_ref[...]
            out_ref[rows, :] = x2


def _ffn(hf, x1, wg, wu, wd, gf, *, final_norm, tm=1024, tf=512, epilogue_rows=32):
    t, d = x1.shape
    dff = wg.shape[1]
    assert t % tm == 0 and dff % tf == 0 and tm % epilogue_rows == 0
    kern = functools.partial(_ffn_kernel, final_norm=final_norm, epilogue_rows=epilogue_rows)
    return pl.pallas_call(
        kern,
        grid=(t // tm, dff // tf),
        in_specs=[
            pl.BlockSpec((tm, d), lambda i, f: (i, 0)),
            pl.BlockSpec(memory_space=pl.ANY),
            pl.BlockSpec((d, tf), lambda i, f: (0, f)),
            pl.BlockSpec((d, tf), lambda i, f: (0, f)),
            pl.BlockSpec((tf, d), lambda i, f: (f, 0)),
            pl.BlockSpec((1, d), lambda i, f: (0, 0)),
        ],
        out_specs=pl.BlockSpec((tm, d), lambda i, f: (i, 0)),
        out_shape=jax.ShapeDtypeStruct((t, d), F32),
        scratch_shapes=[pltpu.VMEM((tm, d), F32), pltpu.SemaphoreType.DMA(())],
        compiler_params=_params(("arbitrary", "arbitrary")),
        name="ffn",
    )(hf, x1, wg, wu, wd, gf)


def kernel(x, norm_mix, w_in, b_igate, b_fgate, conv_w, mh_norm, w_branch_a, w_branch_b,
           w_out, norm_ffn, w_gate, w_up, w_down, norm_final):
    bsz, s_len, d = x.shape
    depth = w_in.shape[0]
    v_width = mh_norm.shape[1]
    qk_width = v_width // 2
    conv_width = conv_w.shape[2]
    n_main = 2 * qk_width + 2 * v_width
    gate_end = n_main + 2 * HEADS
    assert w_in.shape[2] == gate_end + 3 * conv_width + 2 * d
    assert conv_w.shape[1] == CONV_K and b_igate.shape[1] == HEADS and conv_width == v_width
    q_scale = float((qk_width // HEADS) ** -0.5)

    lane = jnp.arange(2 * LANES)[:, None] % LANES
    sel = (lane == jnp.arange(HEADS * LANES)[None, :] // LANES).astype(BF16)
    pad_lanes = lambda a: jnp.pad(a, ((0, 0), (0, LANES - a.shape[1])))

    x2d = x.reshape(bsz * s_len, d)
    for l in range(depth):
        w_main_t, w_gate_t = _wprep(jnp.swapaxes(w_in[l], 0, 1), n_main=n_main, gate_end=gate_end)
        w_gate_col = jnp.concatenate([pad_lanes(w_gate_t[:HEADS].T), pad_lanes(w_gate_t[HEADS:].T)], axis=1)
        bias_lane = jnp.concatenate([pad_lanes(b_igate[l][None, :]), pad_lanes(b_fgate[l][None, :])], axis=1)

        side = (w_branch_a[l], w_branch_b[l], w_out[l], w_gate[l], w_up[l], w_down[l])
        proj, gcol, (wa, wb, wo, wg, wu, wd) = _inproj(
            x2d, norm_mix[l].reshape(1, d), w_main_t, w_gate_col, side, qk_width=qk_width, q_scale=q_scale)
        hm = _mlstm(proj, gcol, bias_lane, mh_norm[l].reshape(1, v_width), sel,
                    bsz=bsz, s_len=s_len, qk_width=qk_width, v_width=v_width)
        x1, hf = _mix(hm, proj, conv_w[l], wa, wb, wo, x2d, norm_ffn[l].reshape(1, d),
                      s_len=s_len, col0=n_main)
        x2d = _ffn(hf, x1, wg, wu, wd, norm_final.reshape(1, d), final_norm=(l == depth - 1))
    return x2d.reshape(bsz, s_len, d)
```
